```python
import jax, jax.numpy as jnp
from jax import lax
import numpy as np

D_MODEL = 1024
BATCH = 2
SEQ = 16384
DEPTH = 2

CTX_LEN = 256
GRID_W = 64
MIX_WIDTH = D_MODEL
RET_HEADS = 4
RET_WIDTH = MIX_WIDTH // 4
RET_DV = RET_WIDTH // RET_HEADS
RET_DK = RET_DV // 2
GLA_HEADS = 4
GLA_WIDTH = MIX_WIDTH // 4
GLA_DV = GLA_WIDTH // GLA_HEADS
GLA_DK = GLA_DV // 2
GLA_RANK = 16
GLA_TAU = 16.0
GQA_WIDTH = MIX_WIDTH - RET_WIDTH - GLA_WIDTH
GQA_HEAD_DIM = 64
GQA_HEADS = GQA_WIDTH // GQA_HEAD_DIM
GQA_KV_HEADS = 2
GQA_GROUP = GQA_HEADS // GQA_KV_HEADS
D_FF = 4 * D_MODEL
SCAN_CHUNK = 64
Q_BLOCK = 128
ROPE_BASE = 10000.0
NORM_EPS = 1e-6
IN_SPLIT_SIZES = (RET_HEADS * RET_DK, RET_HEADS * RET_DK, RET_WIDTH, RET_WIDTH,
                  GLA_HEADS * GLA_DK, GLA_HEADS * GLA_DK, GLA_WIDTH, GLA_WIDTH, 2 * GLA_RANK,
                  GQA_WIDTH, GQA_KV_HEADS * GQA_HEAD_DIM, GQA_KV_HEADS * GQA_HEAD_DIM)
IN_WIDTH = sum(IN_SPLIT_SIZES)

kernel_name = 'hybrid_ret_gla_gqa_prefix_dit_block'


def rms_norm(x, g):
    xf = x.astype(jnp.float32)
    y = xf * lax.rsqrt(jnp.mean(xf * xf, axis=-1, keepdims=True) + NORM_EPS)
    return (y * g.astype(jnp.float32)).astype(x.dtype)


def head_rms(x):
    xf = x.astype(jnp.float32)
    return (xf * lax.rsqrt(jnp.mean(xf * xf, axis=-1, keepdims=True) + NORM_EPS)).astype(x.dtype)


def head_layernorm(x):
    xf = x.astype(jnp.float32)
    mu = jnp.mean(xf, axis=-1, keepdims=True)
    xc = xf - mu
    return (xc * lax.rsqrt(jnp.mean(xc * xc, axis=-1, keepdims=True) + NORM_EPS)).astype(x.dtype)


def modulate(h, shift, scale):
    return h * (1.0 + scale) + shift


def split_heads(a, n_heads):
    return a.reshape(a.shape[:-1] + (n_heads, a.shape[-1] // n_heads))


def _rotate(x, ang):
    x1, x2 = jnp.split(x, 2, axis=-1)
    cos = jnp.cos(ang)[:, None, :]
    sin = jnp.sin(ang)[:, None, :]
    return jnp.concatenate([x1 * cos - x2 * sin, x1 * sin + x2 * cos], axis=-1)


def axial_rope(x, row, col):
    d = x.shape[-1]
    n_freq = d // 4
    inv_freq = ROPE_BASE ** (-jnp.arange(n_freq, dtype=jnp.float32) / n_freq)
    xf = x.astype(jnp.float32)
    x_row, x_col = jnp.split(xf, 2, axis=-1)
    out = jnp.concatenate([_rotate(x_row, row[:, None] * inv_freq),
                           _rotate(x_col, col[:, None] * inv_freq)], axis=-1)
    return out.astype(x.dtype)


def chunk_scan(q, k, v, g, s0, inclusive):
    bsz, length, heads, _ = q.shape
    dv = v.shape[-1]
    n_chunks = length // SCAN_CHUNK

    def to_chunks(a):
        a = a.astype(jnp.float32).reshape(bsz, n_chunks, SCAN_CHUNK, heads, a.shape[-1])
        return a.transpose(1, 0, 3, 2, 4)

    mask = jnp.tril(jnp.ones((SCAN_CHUNK, SCAN_CHUNK), dtype=bool), 0 if inclusive else -1)

    def step(state, inp):
        qc, kc, vc, gc = inp
        b = jnp.cumsum(gc, axis=-2)
        b_last = b[..., -1:, :]
        q_dec = qc * jnp.exp(b)
        scores = jnp.einsum('bhid,bhjd->bhij', q_dec, kc * jnp.exp(-b))
        scores = jnp.where(mask, scores, 0.0)
        out = (jnp.einsum('bhij,bhje->bhie', scores, vc)
               + jnp.einsum('bhid,bhde->bhie', q_dec, state))
        state = (jnp.exp(b_last)[..., 0, :, None] * state
                 + jnp.einsum('bhjd,bhje->bhde', kc * jnp.exp(b_last - b), vc))
        return state, out

    s_final, o = lax.scan(step, s0.astype(jnp.float32),
                          (to_chunks(q), to_chunks(k), to_chunks(v), to_chunks(g)))
    o = o.transpose(1, 0, 3, 2, 4).reshape(bsz, length, heads, dv)
    return o.astype(v.dtype), s_final


def bidir_prefix_scan(q_l, k_l, v_l, gf_l, gb_l, q_c, k_c, v_c, gf_c, gb_c):
    bsz, _, heads, dk = q_l.shape
    dv = v_l.shape[-1]
    zero = jnp.zeros((bsz, heads, dk, dv), jnp.float32)
    rev = lambda a: a[:, ::-1]
    oc_f, sc_f = chunk_scan(q_c, k_c, v_c, gf_c, zero, True)
    ol_f, _ = chunk_scan(q_l, k_l, v_l, gf_l, sc_f, True)
    oc_b, sc_b = chunk_scan(rev(q_c), rev(k_c), rev(v_c), rev(gb_c), zero, False)
    ol_b, _ = chunk_scan(rev(q_l), rev(k_l), rev(v_l), rev(gb_l), sc_b, False)
    return ol_f + rev(ol_b), oc_f + rev(oc_b)


def _attend(q, keys, vals):
    s = jnp.einsum('bkgqd,bksd->bkgqs', q, keys).astype(jnp.float32) * (GQA_HEAD_DIM ** -0.5)
    p = jax.nn.softmax(s, axis=-1).astype(vals.dtype)
    return jnp.einsum('bkgqs,bksd->bkgqd', p, vals)


def gqa_latent(q, k_lat, v_lat, k_ctx, v_ctx):
    bsz, length = q.shape[:2]
    keys = jnp.concatenate([k_ctx, k_lat], axis=1).transpose(0, 2, 1, 3)
    vals = jnp.concatenate([v_ctx, v_lat], axis=1).transpose(0, 2, 1, 3)
    n_blocks = length // Q_BLOCK
    qb = q.reshape(bsz, n_blocks, Q_BLOCK, GQA_KV_HEADS, GQA_GROUP, GQA_HEAD_DIM).transpose(1, 0, 3, 4, 2, 5)
    o = lax.map(lambda qblk: _attend(qblk, keys, vals), qb)
    return o.transpose(1, 0, 4, 2, 3, 5).reshape(bsz, length, GQA_WIDTH)


def gqa_context(q, k, v):
    bsz, n = q.shape[:2]
    qg = q.reshape(bsz, n, GQA_KV_HEADS, GQA_GROUP, GQA_HEAD_DIM).transpose(0, 2, 3, 1, 4)
    o = _attend(qg, k.transpose(0, 2, 1, 3), v.transpose(0, 2, 1, 3))
    return o.transpose(0, 3, 1, 2, 4).reshape(bsz, n, GQA_WIDTH)


def token_mix(h_lat, h_ctx, w_in, w_out, ret_decay_logit, gla_gate_w, gla_gate_b, qk_norm_g, row, col, need_ctx):
    cuts = [int(v) for v in np.cumsum(IN_SPLIT_SIZES)[:-1]]
    pl = jnp.split(h_lat @ w_in, cuts, axis=-1)
    pc = jnp.split(h_ctx @ w_in, cuts, axis=-1)
    bsz, n_lat = h_lat.shape[:2]
    n_ctx = h_ctx.shape[1]

    k_scale = RET_DK ** -0.5
    rq_l = axial_rope(split_heads(pl[0], RET_HEADS), row, col)
    rk_l = axial_rope(split_heads(pl[1], RET_HEADS), row, col) * k_scale
    rv_l = split_heads(pl[2], RET_HEADS)
    rq_c = split_heads(pc[0], RET_HEADS)
    rk_c = split_heads(pc[1], RET_HEADS) * k_scale
    rv_c = split_heads(pc[2], RET_HEADS)
    log_gamma = jax.nn.log_sigmoid(ret_decay_logit.astype(jnp.float32))
    gf_l = jnp.broadcast_to(log_gamma[0][:, None], rq_l.shape)
    gb_l = jnp.broadcast_to(log_gamma[1][:, None], rq_l.shape)
    gf_c = jnp.broadcast_to(log_gamma[0][:, None], rq_c.shape)
    gb_c = jnp.broadcast_to(log_gamma[1][:, None], rq_c.shape)
    ro_l, ro_c = bidir_prefix_scan(rq_l, rk_l, rv_l, gf_l, gb_l, rq_c, rk_c, rv_c, gf_c, gb_c)
    y_ret_l = jax.nn.silu(pl[3]) * head_layernorm(ro_l).reshape(bsz, n_lat, RET_WIDTH)

    def gla_gates(a):
        z_f = a[..., :GLA_RANK] @ gla_gate_w[0] + gla_gate_b[0]
        z_b = a[..., GLA_RANK:] @ gla_gate_w[1] + gla_gate_b[1]
        g_f = jax.nn.log_sigmoid(z_f.astype(jnp.float32)) / GLA_TAU
        g_b = jax.nn.log_sigmoid(z_b.astype(jnp.float32)) / GLA_TAU
        return split_heads(g_f, GLA_HEADS), split_heads(g_b, GLA_HEADS)

    q_scale = GLA_DK ** -0.5
    gq_l = split_heads(pl[4], GLA_HEADS) * q_scale
    gk_l = split_heads(pl[5], GLA_HEADS)
    gv_l = split_heads(pl[6], GLA_HEADS)
    gq_c = split_heads(pc[4], GLA_HEADS) * q_scale
    gk_c = split_heads(pc[5], GLA_HEADS)
    gv_c = split_heads(pc[6], GLA_HEADS)
    ggf_l, ggb_l = gla_gates(pl[8])
    ggf_c, ggb_c = gla_gates(pc[8])
    go_l, go_c = bidir_prefix_scan(gq_l, gk_l, gv_l, ggf_l, ggb_l, gq_c, gk_c, gv_c, ggf_c, ggb_c)
    y_gla_l = jax.nn.silu(pl[7]) * head_rms(go_l).reshape(bsz, n_lat, GLA_WIDTH)

    aq_l = axial_rope(rms_norm(split_heads(pl[9], GQA_HEADS), qk_norm_g[0]), row, col)
    ak_l = axial_rope(rms_norm(split_heads(pl[10], GQA_KV_HEADS), qk_norm_g[1]), row, col)
    av_l = split_heads(pl[11], GQA_KV_HEADS)
    ak_c = rms_norm(split_heads(pc[10], GQA_KV_HEADS), qk_norm_g[1])
    av_c = split_heads(pc[11], GQA_KV_HEADS)
    y_gqa_l = gqa_latent(aq_l, ak_l, av_l, ak_c, av_c)

    y_lat = jnp.concatenate([y_ret_l, y_gla_l, y_gqa_l], axis=-1) @ w_out
    if not need_ctx:
        return y_lat, None
    aq_c = rms_norm(split_heads(pc[9], GQA_HEADS), qk_norm_g[0])
    y_ret_c = jax.nn.silu(pc[3]) * head_layernorm(ro_c).reshape(bsz, n_ctx, RET_WIDTH)
    y_gla_c = jax.nn.silu(pc[7]) * head_rms(go_c).reshape(bsz, n_ctx, GLA_WIDTH)
    y_gqa_c = gqa_context(aq_c, ak_c, av_c)
    y_ctx = jnp.concatenate([y_ret_c, y_gla_c, y_gqa_c], axis=-1) @ w_out
    return y_lat, y_ctx


def sq_relu_mlp(h, w1, w2):
    return jnp.square(jax.nn.relu(h @ w1)) @ w2


def setup_inputs(seed: int = 0) -> dict:
    key = jax.random.key(seed)
    ks = jax.random.split(key, 20)
    nrm = lambda k, shape, s: jax.random.normal(k, shape, jnp.float32) * s
    ret_logit0 = jnp.asarray(np.log(2.0 ** (5.0 + np.arange(RET_HEADS)) - 1.0), dtype=jnp.float32)
    return {
        'x': nrm(ks[0], (BATCH, SEQ, D_MODEL), 1.0),
        'c': nrm(ks[1], (BATCH, D_MODEL), 1.0),
        'ctx': nrm(ks[2], (BATCH, CTX_LEN, D_MODEL), 1.0),
        'c_ctx': nrm(ks[3], (D_MODEL,), 1.0),
        'mod_w': nrm(ks[4], (DEPTH, D_MODEL, 6 * D_MODEL), 0.5 * D_MODEL ** -0.5),
        'mod_b': nrm(ks[5], (DEPTH, 6 * D_MODEL), 0.01),
        'attn_norm_g': 1.0 + nrm(ks[6], (DEPTH, D_MODEL), 0.05),
        'mlp_norm_g': 1.0 + nrm(ks[7], (DEPTH, D_MODEL), 0.05),
        'w_in': nrm(ks[8], (DEPTH, D_MODEL, IN_WIDTH), D_MODEL ** -0.5),
        'w_out': nrm(ks[9], (DEPTH, MIX_WIDTH, D_MODEL), MIX_WIDTH ** -0.5),
        'ret_decay_logit': ret_logit0 + nrm(ks[10], (DEPTH, 2, RET_HEADS), 0.01),
        'gla_gate_w': nrm(ks[11], (DEPTH, 2, GLA_RANK, GLA_HEADS * GLA_DK), GLA_RANK ** -0.5),
        'gla_gate_b': nrm(ks[12], (DEPTH, 2, GLA_HEADS * GLA_DK), 0.1),
        'qk_norm_g': 1.0 + nrm(ks[13], (DEPTH, 2, GQA_HEAD_DIM), 0.05),
        'mlp_w1': nrm(ks[14], (DEPTH, D_MODEL, D_FF), D_MODEL ** -0.5),
        'mlp_w2': nrm(ks[15], (DEPTH, D_FF, D_MODEL), D_FF ** -0.5),
        'final_norm_g': 1.0 + nrm(ks[16], (D_MODEL,), 0.05),
    }


def reference(x, c, ctx, c_ctx, mod_w, mod_b, attn_norm_g, mlp_norm_g, w_in, w_out, ret_decay_logit,
              gla_gate_w, gla_gate_b, qk_norm_g, mlp_w1, mlp_w2, final_norm_g):
    n_lat = x.shape[1]
    rows = n_lat // GRID_W
    row = jnp.repeat(jnp.arange(rows, dtype=jnp.float32), GRID_W)
    col = (jnp.arange(rows * GRID_W) % GRID_W).astype(jnp.float32)
    silu_c = jax.nn.silu(c)
    silu_cc = jax.nn.silu(c_ctx)
    for i in range(DEPTH):
        need_ctx = i < DEPTH - 1
        sh_a, sc_a, gt_a, sh_m, sc_m, gt_m = [m[:, None, :] for m in
                                              jnp.split(silu_c @ mod_w[i] + mod_b[i], 6, axis=-1)]
        csh_a, csc_a, cgt_a, csh_m, csc_m, cgt_m = jnp.split(silu_cc @ mod_w[i] + mod_b[i], 6, axis=-1)
        h_lat = modulate(rms_norm(x, attn_norm_g[i]), sh_a, sc_a)
        h_ctx = modulate(rms_norm(ctx, attn_norm_g[i]), csh_a, csc_a)
        y_lat, y_ctx = token_mix(h_lat, h_ctx, w_in[i], w_out[i], ret_decay_logit[i], gla_gate_w[i],
                                 gla_gate_b[i], qk_norm_g[i], row, col, need_ctx)
        x = x + gt_a * y_lat
        x = x + gt_m * sq_relu_mlp(modulate(rms_norm(x, mlp_norm_g[i]), sh_m, sc_m), mlp_w1[i], mlp_w2[i])
        if need_ctx:
            ctx = ctx + cgt_a * y_ctx
            ctx = ctx + cgt_m * sq_relu_mlp(modulate(rms_norm(ctx, mlp_norm_g[i]), csh_m, csc_m),
                                            mlp_w1[i], mlp_w2[i])
    return rms_norm(x, final_norm_g)
```

```python
import functools

import jax
import jax.numpy as jnp
import numpy as np
from jax import lax
from jax.experimental import pallas as pl
from jax.experimental.pallas import tpu as pltpu

F32 = jnp.float32
BF16 = jnp.bfloat16

GRID_W = 64
RET_HEADS = 4
RET_DK = 32
RET_DV = 64
GLA_HEADS = 4
GLA_DK = 32
GLA_DV = 64
GLA_RANK = 16
GLA_TAU = 16.0
GQA_HEADS = 8
GQA_KV_HEADS = 2
GQA_GROUP = 4
GQA_HEAD_DIM = 64
SCAN_CHUNK = 64
ROPE_BASE = 10000.0
NORM_EPS = 1e-6

V7X_VMEM_LIMIT_BYTES = 56 * 1024 * 1024
V_ROWS = 80

NAT_WIDTH = 1664
TR_WIDTH = 768


def _silu(x):
    return x * (1.0 / (1.0 + jnp.exp(-x)))


def _log_sigmoid(z):
    return jnp.minimum(z, 0.0) - jnp.log(1.0 + jnp.exp(-jnp.abs(z)))


def _dot(a, b):
    return jnp.dot(a, b, preferred_element_type=F32)


def _dot_nt(a, b):
    return lax.dot_general(a, b, (((1,), (1,)), ((), ())), preferred_element_type=F32)


def _dot_tn(a, b):
    return lax.dot_general(a, b, (((0,), (0,)), ((), ())), preferred_element_type=F32)


def _mod_kernel(c_ref, w_ref, b_ref, o_ref):
    s = _silu(c_ref[...]).astype(BF16)
    o_ref[0] = _dot(s, w_ref[0].astype(BF16)) + b_ref[0]


def _modulation(cvec, mod_w, mod_b):
    depth, d, n = mod_w.shape
    tn = 1536
    return pl.pallas_call(
        _mod_kernel,
        grid=(depth, n // tn),
        in_specs=[pl.BlockSpec((8, d), lambda l, j: (0, 0)),
                  pl.BlockSpec((1, d, tn), lambda l, j: (l, 0, j)),
                  pl.BlockSpec((1, 1, tn), lambda l, j: (l, 0, j))],
        out_specs=pl.BlockSpec((1, 8, tn), lambda l, j: (l, 0, j)),
        out_shape=jax.ShapeDtypeStruct((depth, 8, n), F32),
        compiler_params=pltpu.CompilerParams(dimension_semantics=("parallel", "parallel"),
                                             vmem_limit_bytes=V7X_VMEM_LIMIT_BYTES),
        name="modulation",
    )(cvec, mod_w, mod_b.reshape(depth, 1, n))


def _rope_nat(x, cos, sin_signed, nf):
    lane = lax.broadcasted_iota(jnp.int32, x.shape, 1)
    first = (lane % (2 * nf)) < nf
    partner = jnp.where(first, pltpu.roll(x, 128 - nf, 1), pltpu.roll(x, nf, 1))
    return x * cos + partner * sin_signed


def _head_norm_rope_t(xt, g_col, cos_t, sin_t, rope):
    h = xt.shape[0] // GQA_HEAD_DIM
    x3 = xt.reshape(h, GQA_HEAD_DIM, xt.shape[1])
    ms = jnp.sum(x3 * x3, axis=1, keepdims=True) * (1.0 / GQA_HEAD_DIM)
    x3 = x3 * lax.rsqrt(ms + NORM_EPS) * g_col[None]
    if rope:
        swapped = jnp.concatenate([x3[:, 16:32], x3[:, 0:16], x3[:, 48:64], x3[:, 32:48]], axis=1)
        x3 = x3 * cos_t[None] + swapped * sin_t[None]
    return x3


def _in_proj_kernel(x_ref, mod_ref, g_ref, wn_ref, wt_ref, gw_ref, gb_ref, qg_ref, kg_ref,
                    cr_ref, sr_ref, ct_ref, st_ref,
                    rq_ref, rk_ref, rv_ref, rg_ref, gq_ref, gk_ref, gv_ref, gg_ref, gf_ref, gbw_ref,
                    aq_ref, ak_ref, av_ref, *, rope):
    x = x_ref[0]
    ms = jnp.mean(x * x, axis=-1, keepdims=True)
    h = x * lax.rsqrt(ms + NORM_EPS) * g_ref[...]
    h = h * (1.0 + mod_ref[0, 1:2, :]) + mod_ref[0, 0:1, :]
    hb = h.astype(BF16)
    p = _dot(hb, wn_ref[...])
    pt = _dot_nt(wt_ref[...], hb)

    rq = p[:, 0:128]
    rk = p[:, 128:256]
    gq = p[:, 768:896]
    if rope:
        cr = cr_ref[...]
        sr = sr_ref[...]
        rq = _rope_nat(rq, cr, sr, RET_DK // 4)
        rk = _rope_nat(rk, cr, sr, RET_DK // 4)
    rq_ref[0] = rq
    rk_ref[0] = rk * (RET_DK ** -0.5)
    rv_ref[0] = p[:, 256:512]
    rg_ref[0] = _silu(p[:, 512:768])
    gq_ref[0] = gq * (GLA_DK ** -0.5)
    gk_ref[0] = p[:, 896:1024]
    gv_ref[0] = p[:, 1024:1280]
    gg_ref[0] = _silu(p[:, 1280:1536])
    z = _dot(p[:, 1536:1664].astype(BF16), gw_ref[...]) + gb_ref[...]
    gates = _log_sigmoid(z) * (1.0 / GLA_TAU)
    gf_ref[0] = gates[:, 0:128]
    gbw_ref[0] = gates[:, 128:256]

    ct = ct_ref[...]
    st = st_ref[...]
    q3 = _head_norm_rope_t(pt[0:512], qg_ref[...], ct, st, rope)
    aq_ref[0] = (q3 * (GQA_HEAD_DIM ** -0.5)).astype(BF16)
    k3 = _head_norm_rope_t(pt[512:640], kg_ref[...], ct, st, rope)
    ak_ref[0] = k3.reshape(GQA_KV_HEADS * GQA_HEAD_DIM, k3.shape[2]).T.astype(BF16)
    av_ref[0] = pt[640:768].reshape(GQA_KV_HEADS, GQA_HEAD_DIM, pt.shape[1]).astype(BF16)


def _in_proj(x, mod, norm_g, w_nat, w_tr, gate_w_bd, gate_b_cat, q_gain, k_gain, tables, *, rope, tm):
    bsz, n, d = x.shape
    tm = min(tm, n)
    cos_r, sin_r, cos_t, sin_t = tables
    tok = lambda w: pl.BlockSpec((1, tm, w), lambda b, i: (b, i, 0))
    full = lambda a: pl.BlockSpec(a.shape, lambda b, i: (0,) * a.ndim)
    f32_out = lambda w: jax.ShapeDtypeStruct((bsz, n, w), F32)
    outs = pl.pallas_call(
        functools.partial(_in_proj_kernel, rope=rope),
        grid=(bsz, n // tm),
        in_specs=[tok(d),
                  pl.BlockSpec((1, 6, d), lambda b, i: (b, 0, 0)),
                  full(norm_g), full(w_nat), full(w_tr), full(gate_w_bd), full(gate_b_cat),
                  full(q_gain), full(k_gain),
                  pl.BlockSpec((tm, 128), lambda b, i: (i, 0)),
                  pl.BlockSpec((tm, 128), lambda b, i: (i, 0)),
                  pl.BlockSpec((GQA_HEAD_DIM, tm), lambda b, i: (0, i)),
                  pl.BlockSpec((GQA_HEAD_DIM, tm), lambda b, i: (0, i))],
        out_specs=[tok(128), tok(128), tok(256), tok(256), tok(128), tok(128), tok(256), tok(256),
                   tok(128), tok(128),
                   pl.BlockSpec((1, GQA_HEADS, GQA_HEAD_DIM, tm), lambda b, i: (b, 0, 0, i)),
                   tok(128),
                   pl.BlockSpec((1, GQA_KV_HEADS, GQA_HEAD_DIM, tm), lambda b, i: (b, 0, 0, i))],
        out_shape=[f32_out(128), f32_out(128), f32_out(256), f32_out(256),
                   f32_out(128), f32_out(128), f32_out(256), f32_out(256), f32_out(128), f32_out(128),
                   jax.ShapeDtypeStruct((bsz, GQA_HEADS, GQA_HEAD_DIM, n), BF16),
                   jax.ShapeDtypeStruct((bsz, n, 128), BF16),
                   jax.ShapeDtypeStruct((bsz, GQA_KV_HEADS, GQA_HEAD_DIM, n), BF16)],
        compiler_params=pltpu.CompilerParams(dimension_semantics=("parallel", "parallel"),
                                             vmem_limit_bytes=V7X_VMEM_LIMIT_BYTES),
        name="in_proj_rope" if rope else "in_proj_ctx",
    )(x, mod, norm_g, w_nat, w_tr, gate_w_bd, gate_b_cat, q_gain, k_gain, cos_r, sin_r, cos_t, sin_t)
    return outs


def _cumsum_rows(g, reverse):
    c = g.shape[0]
    row = lax.broadcasted_iota(jnp.int32, g.shape, 0)
    b = g
    s = 1
    while s < c:
        if reverse:
            b = b + jnp.where(row < c - s, pltpu.roll(b, c - s, 0), 0.0)
        else:
            b = b + jnp.where(row >= s, pltpu.roll(b, s, 0), 0.0)
        s *= 2
    return b


def _scan_chunk(q, k, v, b, b_end, st_ref, reverse):
    c = q.shape[0]
    qd = (q * jnp.exp(b)).astype(BF16)
    kd = (k * jnp.exp(-b)).astype(BF16)
    ku = (k * jnp.exp(b_end - b)).astype(BF16)
    vb = v.astype(BF16)

    lane_k = lax.broadcasted_iota(jnp.int32, (c, 128), 1) // RET_DK
    lane_v = lax.broadcasted_iota(jnp.int32, (c, 256), 1) // RET_DV
    zk = jnp.zeros_like(kd)
    zv = jnp.zeros_like(vb)
    kd_stack = jnp.concatenate([jnp.where(lane_k == h, kd, zk) for h in range(RET_HEADS)], axis=0)
    v_stack = jnp.concatenate([jnp.where(lane_v == h, vb, zv) for h in range(RET_HEADS)], axis=0)

    a = _dot_nt(qd, kd_stack)
    i = lax.broadcasted_iota(jnp.int32, a.shape, 0)
    j = lax.broadcasted_iota(jnp.int32, a.shape, 1) % c
    keep = (j > i) if reverse else (j <= i)
    a = jnp.where(keep, a, 0.0).astype(BF16)

    st = st_ref[...]
    o = _dot(a, v_stack) + _dot_nt(qd, st.astype(BF16))

    u = _dot_tn(vb, ku)
    r = lax.broadcasted_iota(jnp.int32, u.shape, 0) // RET_DV
    l = lax.broadcasted_iota(jnp.int32, u.shape, 1) // RET_DK
    st_ref[...] = st * jnp.exp(b_end) + jnp.where(r == l, u, 0.0)
    return o


def _scan_kernel(*refs, gated, tile):
    if gated:
        (qf_ref, kf_ref, vf_ref, gf_ref, qb_ref, kb_ref, vb_ref, gb_ref, of_ref, ob_ref, st_ref) = refs
    else:
        (qf_ref, kf_ref, vf_ref, qb_ref, kb_ref, vb_ref, lg_ref, of_ref, ob_ref, st_ref) = refs
    c = SCAN_CHUNK

    @pl.when(pl.program_id(1) == 0)
    def _():
        st_ref[...] = jnp.zeros_like(st_ref)

    n_chunks = tile // c
    for direction in range(2):
        reverse = direction == 1
        q_ref, k_ref, v_ref, o_ref = ((qb_ref, kb_ref, vb_ref, ob_ref) if reverse
                                      else (qf_ref, kf_ref, vf_ref, of_ref))
        order = range(n_chunks - 1, -1, -1) if reverse else range(n_chunks)
        for ci in order:
            rows = pl.ds(ci * c, c)
            if gated:
                g = (gb_ref if reverse else gf_ref)[0, rows, :]
                b = _cumsum_rows(g, reverse)
                b_end = b[0:1, :] if reverse else b[c - 1:c, :]
            else:
                lg = _log_sigmoid(lg_ref[direction:direction + 1, :])
                row = lax.broadcasted_iota(jnp.int32, (c, 128), 0)
                steps = (c - row) if reverse else (row + 1)
                b = steps.astype(F32) * lg
                b_end = float(c) * lg
            o_ref[0, rows, :] = _scan_chunk(q_ref[0, rows, :], k_ref[0, rows, :], v_ref[0, rows, :],
                                            b, b_end, st_ref.at[direction], reverse)


def _bidir_scan(q, k, v, gates, decay_logit, *, n_lat, tile):
    bsz, n, _ = q.shape
    n_tiles = n // tile
    n_lat_tiles = n_lat // tile
    assert n_tiles == n_lat_tiles + 1, "context must be exactly one scan tile"
    fwd = lambda b, s: (b, jnp.where(s == 0, n_lat_tiles, s - 1), 0)
    bwd = lambda b, s: (b, jnp.where(s == 0, n_lat_tiles, n_lat_tiles - s), 0)
    spec = lambda w, im: pl.BlockSpec((1, tile, w), im)
    gated = gates is not None
    if gated:
        args = (q, k, v, gates[0], q, k, v, gates[1])
        in_specs = [spec(128, fwd), spec(128, fwd), spec(256, fwd), spec(128, fwd),
                    spec(128, bwd), spec(128, bwd), spec(256, bwd), spec(128, bwd)]
    else:
        args = (q, k, v, q, k, v, decay_logit)
        in_specs = [spec(128, fwd), spec(128, fwd), spec(256, fwd),
                    spec(128, bwd), spec(128, bwd), spec(256, bwd),
                    pl.BlockSpec(decay_logit.shape, lambda b, s: (0, 0))]
    return pl.pallas_call(
        functools.partial(_scan_kernel, gated=gated, tile=tile),
        grid=(bsz, n_tiles),
        in_specs=in_specs,
        out_specs=[spec(256, fwd), spec(256, bwd)],
        out_shape=[jax.ShapeDtypeStruct((bsz, n, 256), F32)] * 2,
        scratch_shapes=[pltpu.VMEM((2, 256, 128), F32)],
        compiler_params=pltpu.CompilerParams(dimension_semantics=("parallel", "arbitrary"),
                                             vmem_limit_bytes=V7X_VMEM_LIMIT_BYTES),
        name="scan_gla" if gated else "scan_ret",
    )(*args)


def _attn_kernel(q_ref, k_ref, v_ref, o_ref, acc_ref, *, n_kv_tiles):
    kv = pl.program_id(1)
    tq = q_ref.shape[3]
    qcat = jnp.concatenate([q_ref[0, g] for g in range(GQA_GROUP)], axis=1)
    zero = jnp.zeros_like(qcat)
    qpad = jnp.where(kv == 0, jnp.concatenate([qcat, zero], axis=0), jnp.concatenate([zero, qcat], axis=0))

    acc_ref[...] = jnp.zeros_like(acc_ref)

    def body(j, m_prev):
        s = _dot(k_ref[0, j], qpad)
        m_new = jnp.maximum(m_prev, jnp.max(s, axis=0, keepdims=True))
        alpha = jnp.exp(m_prev - m_new)
        p = jnp.exp(s - m_new).astype(BF16)
        acc_ref[...] = acc_ref[...] * alpha + _dot(v_ref[0, 0, j], p)
        return m_new

    lax.fori_loop(0, n_kv_tiles, body, jnp.full((1, GQA_GROUP * tq), -jnp.inf, F32))

    acc = acc_ref[...]
    o = acc[0:GQA_HEAD_DIM] * (1.0 / acc[GQA_HEAD_DIM:GQA_HEAD_DIM + 1])
    o_ref[0] = jnp.concatenate([o[:, g * tq:(g + 1) * tq].T for g in range(GQA_GROUP)], axis=1).astype(o_ref.dtype)


def _attention(q_t, k_nat, v_t, *, tq, tk):
    bsz, _, _, nq = q_t.shape
    ns = k_nat.shape[1]
    tq = min(tq, nq)
    tk = min(tk, ns)
    n_kv_tiles = ns // tk
    k_tiles = k_nat.reshape(bsz, n_kv_tiles, tk, 128)
    ones = jnp.ones((bsz, GQA_KV_HEADS, V_ROWS - GQA_HEAD_DIM, ns), BF16)
    v_tiles = jnp.concatenate([v_t, ones], axis=2).reshape(bsz, GQA_KV_HEADS, V_ROWS, n_kv_tiles, tk)
    v_tiles = v_tiles.transpose(0, 1, 3, 2, 4)
    return pl.pallas_call(
        functools.partial(_attn_kernel, n_kv_tiles=n_kv_tiles),
        grid=(bsz, GQA_KV_HEADS, nq // tq),
        in_specs=[pl.BlockSpec((1, GQA_GROUP, GQA_HEAD_DIM, tq), lambda b, h, i: (b, h, 0, i)),
                  pl.BlockSpec((1, n_kv_tiles, tk, 128), lambda b, h, i: (b, 0, 0, 0)),
                  pl.BlockSpec((1, 1, n_kv_tiles, V_ROWS, tk), lambda b, h, i: (b, h, 0, 0, 0))],
        out_specs=pl.BlockSpec((1, tq, GQA_GROUP * GQA_HEAD_DIM), lambda b, h, i: (b, i, h)),
        out_shape=jax.ShapeDtypeStruct((bsz, nq, GQA_HEADS * GQA_HEAD_DIM), BF16),
        scratch_shapes=[pltpu.VMEM((V_ROWS, GQA_GROUP * tq), F32)],
        compiler_params=pltpu.CompilerParams(dimension_semantics=("parallel", "parallel", "parallel"),
                                             vmem_limit_bytes=V7X_VMEM_LIMIT_BYTES),
        name="gqa_attention",
    )(q_t, k_tiles, v_tiles)


def _group_mean(x, gmat):
    hi = x.astype(BF16)
    lo = (x - hi.astype(F32)).astype(BF16)
    return (_dot(hi, gmat) + _dot(lo, gmat)) * (1.0 / RET_DV)


def _out_proj_kernel(x_ref, mod_ref, rf_ref, rb_ref, rg_ref, gf_ref, gb_ref, gg_ref, ya_ref, w_ref, o_ref):
    r = lax.broadcasted_iota(jnp.int32, (256, 256), 0) // RET_DV
    c = lax.broadcasted_iota(jnp.int32, (256, 256), 1) // RET_DV
    gmat = jnp.where(r == c, 1.0, 0.0).astype(BF16)

    ro = rf_ref[0] + rb_ref[0]
    rc = ro - _group_mean(ro, gmat)
    y_ret = rg_ref[0] * (rc * lax.rsqrt(_group_mean(rc * rc, gmat) + NORM_EPS))
    go = gf_ref[0] + gb_ref[0]
    y_gla = gg_ref[0] * (go * lax.rsqrt(_group_mean(go * go, gmat) + NORM_EPS))
    y = jnp.concatenate([y_ret.astype(BF16), y_gla.astype(BF16), ya_ref[0]], axis=1)
    o_ref[0] = x_ref[0] + mod_ref[0, 2:3, :] * _dot(y, w_ref[...])


def _out_proj(x, mod, scans, gates, y_attn, w_out, *, tm, tok_offset):
    bsz, n, d = x.shape
    tm = min(tm, n)
    tok = lambda w: pl.BlockSpec((1, tm, w), lambda b, i: (b, i, 0))
    off = lambda w: pl.BlockSpec((1, tm, w), lambda b, i: (b, i + tok_offset, 0))
    rf, rb, gf, gb = scans
    rg, gg = gates
    return pl.pallas_call(
        _out_proj_kernel,
        grid=(bsz, n // tm),
        in_specs=[tok(d), pl.BlockSpec((1, 6, d), lambda b, i: (b, 0, 0)),
                  off(256), off(256), tok(256), off(256), off(256), tok(256), tok(512),
                  pl.BlockSpec(w_out.shape, lambda b, i: (0, 0))],
        out_specs=tok(d),
        out_shape=jax.ShapeDtypeStruct((bsz, n, d), F32),
        compiler_params=pltpu.CompilerParams(dimension_semantics=("parallel", "parallel"),
                                             vmem_limit_bytes=V7X_VMEM_LIMIT_BYTES),
        name="out_proj",
    )(x, mod, rf, rb, rg, gf, gb, gg, y_attn, w_out)


def _mlp_kernel(x_ref, mod_ref, g_ref, w1_ref, w2_ref, fg_ref, o_ref, *, ff_tile, final_norm):
    x = x_ref[0]
    ms = jnp.mean(x * x, axis=-1, keepdims=True)
    h = x * lax.rsqrt(ms + NORM_EPS) * g_ref[...]
    hb = (h * (1.0 + mod_ref[0, 4:5, :]) + mod_ref[0, 3:4, :]).astype(BF16)
    d_ff = w1_ref.shape[1]
    acc = jnp.zeros(x.shape, F32)
    for c in range(d_ff // ff_tile):
        cols = slice(c * ff_tile, (c + 1) * ff_tile)
        a = jnp.maximum(_dot(hb, w1_ref[:, cols]), 0.0)
        acc = acc + _dot((a * a).astype(BF16), w2_ref[cols, :])
    y = x + mod_ref[0, 5:6, :] * acc
    if final_norm:
        y = y * lax.rsqrt(jnp.mean(y * y, axis=-1, keepdims=True) + NORM_EPS) * fg_ref[...]
    o_ref[0] = y


def _mlp(x, mod, norm_g, w1, w2, final_g, *, tm, final_norm):
    bsz, n, d = x.shape
    tm = min(tm, n)
    tok = pl.BlockSpec((1, tm, d), lambda b, i: (b, i, 0))
    full = lambda a: pl.BlockSpec(a.shape, lambda b, i: (0,) * a.ndim, pipeline_mode=pl.Buffered(1))
    return pl.pallas_call(
        functools.partial(_mlp_kernel, ff_tile=1024, final_norm=final_norm),
        grid=(bsz, n // tm),
        in_specs=[tok, pl.BlockSpec((1, 6, d), lambda b, i: (b, 0, 0)),
                  full(norm_g), full(w1), full(w2), full(final_g)],
        out_specs=tok,
        out_shape=jax.ShapeDtypeStruct((bsz, n, d), F32),
        compiler_params=pltpu.CompilerParams(dimension_semantics=("parallel", "parallel"),
                                             vmem_limit_bytes=V7X_VMEM_LIMIT_BYTES),
        name="mlp_final" if final_norm else "mlp",
    )(x, mod, norm_g, w1, w2, final_g)


def _prep_w_in(w):
    d = w.shape[0]
    ret = w[:, 0:768]
    gla = w[:, 768:1536]
    low = w[:, 1536:1568]
    aq = w[:, 1568:2080]
    ak = w[:, 2080:2208]
    av = w[:, 2208:2336]
    w_nat = jnp.concatenate([ret, gla, low, jnp.zeros((d, 128 - 2 * GLA_RANK), w.dtype)], axis=1)
    w_tr = jnp.concatenate([aq, ak, av], axis=1).T
    return w_nat.astype(BF16), w_tr.astype(BF16)


def _rope_tables(n_lat):
    t = jnp.arange(n_lat)
    row = (t // GRID_W).astype(F32)
    col = (t % GRID_W).astype(F32)

    def head_tables(head_dim):
        nf = head_dim // 4
        inv_freq = ROPE_BASE ** (-jnp.arange(nf, dtype=F32) / nf)
        ang_r = row[:, None] * inv_freq
        ang_c = col[:, None] * inv_freq
        cos = jnp.concatenate([jnp.cos(ang_r)] * 2 + [jnp.cos(ang_c)] * 2, axis=1)
        sin = jnp.concatenate([-jnp.sin(ang_r), jnp.sin(ang_r), -jnp.sin(ang_c), jnp.sin(ang_c)], axis=1)
        return cos, sin

    cos_r, sin_r = head_tables(RET_DK)
    cos_r = jnp.tile(cos_r, (1, RET_HEADS))
    sin_r = jnp.tile(sin_r, (1, RET_HEADS))
    cos_a, sin_a = head_tables(GQA_HEAD_DIM)
    return cos_r, sin_r, cos_a.T, sin_a.T


def kernel(x, c, ctx, c_ctx, mod_w, mod_b, attn_norm_g, mlp_norm_g, w_in, w_out, ret_decay_logit,
           gla_gate_w, gla_gate_b, qk_norm_g, mlp_w1, mlp_w2, final_norm_g):
    bsz, n_lat, d = x.shape
    n_ctx = ctx.shape[1]
    depth = mod_w.shape[0]
    scan_tile = n_ctx
    assert n_lat % scan_tile == 0 and scan_tile % SCAN_CHUNK == 0 and bsz <= 7

    cvec = jnp.zeros((8, d), F32).at[:bsz].set(c).at[bsz].set(c_ctx)
    mods = _modulation(cvec, mod_w, mod_b)
    lat_tables = _rope_tables(n_lat)
    ctx_tables = tuple(jnp.zeros((n_ctx, 128), F32) for _ in range(2)) + \
        tuple(jnp.zeros((GQA_HEAD_DIM, n_ctx), F32) for _ in range(2))
    final_g = final_norm_g.reshape(1, d)

    for i in range(depth):
        need_ctx = i < depth - 1
        mod_lat = mods[i, :bsz].reshape(bsz, 6, d)
        mod_ctx = jnp.broadcast_to(mods[i, bsz].reshape(1, 6, d), (bsz, 6, d))
        w_nat, w_tr = _prep_w_in(w_in[i])
        gw = jnp.zeros((128, 256), F32)
        gw = gw.at[0:GLA_RANK, 0:128].set(gla_gate_w[i, 0]).at[GLA_RANK:2 * GLA_RANK, 128:256].set(gla_gate_w[i, 1])
        gw = gw.astype(BF16)
        gb = gla_gate_b[i].reshape(1, 256)
        q_gain = qk_norm_g[i, 0].reshape(GQA_HEAD_DIM, 1)
        k_gain = qk_norm_g[i, 1].reshape(GQA_HEAD_DIM, 1)
        attn_g = attn_norm_g[i].reshape(1, d)
        mlp_g = mlp_norm_g[i].reshape(1, d)
        decay = jnp.repeat(ret_decay_logit[i], RET_DK, axis=1)
        w_o = w_out[i].astype(BF16)
        w1 = mlp_w1[i].astype(BF16)
        w2 = mlp_w2[i].astype(BF16)

        pl_lat = _in_proj(x, mod_lat, attn_g, w_nat, w_tr, gw, gb, q_gain, k_gain, lat_tables, rope=True, tm=512)
        pl_ctx = _in_proj(ctx, mod_ctx, attn_g, w_nat, w_tr, gw, gb, q_gain, k_gain, ctx_tables, rope=False,
                          tm=n_ctx)
        cat = lambda j, axis=1: jnp.concatenate([pl_lat[j], pl_ctx[j]], axis=axis)
        rq, rk, rv = cat(0), cat(1), cat(2)
        gq, gk, gv, gf, gbw = cat(4), cat(5), cat(6), cat(8), cat(9)
        ak = cat(11)
        av = cat(12, axis=3)

        ret_f, ret_b = _bidir_scan(rq, rk, rv, None, decay, n_lat=n_lat, tile=scan_tile)
        gla_f, gla_b = _bidir_scan(gq, gk, gv, (gf, gbw), None, n_lat=n_lat, tile=scan_tile)
        y_attn = _attention(pl_lat[10], ak, av, tq=256, tk=256)

        scans = (ret_f, ret_b, gla_f, gla_b)
        x = _out_proj(x, mod_lat, scans, (pl_lat[3], pl_lat[7]), y_attn, w_o, tm=scan_tile, tok_offset=0)
        x = _mlp(x, mod_lat, mlp_g, w1, w2, final_g, tm=512, final_norm=not need_ctx)
        if need_ctx:
            y_attn_c = _attention(pl_ctx[10], pl_ctx[11], pl_ctx[12], tq=256, tk=256)
            ctx = _out_proj(ctx, mod_ctx, scans, (pl_ctx[3], pl_ctx[7]), y_attn_c, w_o, tm=scan_tile,
                            tok_offset=n_lat // scan_tile)
            ctx = _mlp(ctx, mod_ctx, mlp_g, w1, w2, final_g, tm=n_ctx, final_norm=False)
    return x
```

```python
import functools

import jax
import jax.numpy as jnp
import numpy as np
from jax import lax
from jax.experimental import pallas as pl
from jax.experimental.pallas import tpu as pltpu

F32 = jnp.float32
BF16 = jnp.bfloat16

GRID_W = 64
RET_HEADS = 4
RET_DK = 32
RET_DV = 64
GLA_HEADS = 4
GLA_DK = 32
GLA_DV = 64
GLA_RANK = 16
GLA_TAU = 16.0
GQA_HEADS = 8
GQA_KV_HEADS = 2
GQA_GROUP = 4
GQA_HEAD_DIM = 64
SCAN_CHUNK = 64
ROPE_BASE = 10000.0
NORM_EPS = 1e-6

V7X_VMEM_LIMIT_BYTES = 56 * 1024 * 1024
V_ROWS = 80
ATTN_SUB_KEYS = 128
ATTN_COL_BLOCK = 256
ATTN_LOOKAHEAD = 12
LOG2_E = 1.4426950408889634

NAT_WIDTH = 1664
TR_WIDTH = 768


def _silu(x):
    return x * (1.0 / (1.0 + jnp.exp(-x)))


def _log_sigmoid(z):
    return jnp.minimum(z, 0.0) - jnp.log(1.0 + jnp.exp(-jnp.abs(z)))


def _dot(a, b):
    return jnp.dot(a, b, preferred_element_type=F32)


def _dot_nt(a, b):
    return lax.dot_general(a, b, (((1,), (1,)), ((), ())), preferred_element_type=F32)


def _dot_tn(a, b):
    return lax.dot_general(a, b, (((0,), (0,)), ((), ())), preferred_element_type=F32)


def _mod_kernel(c_ref, w_ref, b_ref, o_ref):
    s = _silu(c_ref[...]).astype(BF16)
    o_ref[0] = _dot(s, w_ref[0].astype(BF16)) + b_ref[0]


def _modulation(cvec, mod_w, mod_b):
    depth, d, n = mod_w.shape
    tn = 1536
    return pl.pallas_call(
        _mod_kernel,
        grid=(depth, n // tn),
        in_specs=[pl.BlockSpec((8, d), lambda l, j: (0, 0)),
                  pl.BlockSpec((1, d, tn), lambda l, j: (l, 0, j)),
                  pl.BlockSpec((1, 1, tn), lambda l, j: (l, 0, j))],
        out_specs=pl.BlockSpec((1, 8, tn), lambda l, j: (l, 0, j)),
        out_shape=jax.ShapeDtypeStruct((depth, 8, n), F32),
        compiler_params=pltpu.CompilerParams(dimension_semantics=("parallel", "parallel"),
                                             vmem_limit_bytes=V7X_VMEM_LIMIT_BYTES),
        name="modulation",
    )(cvec, mod_w, mod_b.reshape(depth, 1, n))


def _rope_nat(x, cos, sin_signed, nf):
    lane = lax.broadcasted_iota(jnp.int32, x.shape, 1)
    first = (lane % (2 * nf)) < nf
    partner = jnp.where(first, pltpu.roll(x, 128 - nf, 1), pltpu.roll(x, nf, 1))
    return x * cos + partner * sin_signed


def _head_norm_rope_t(xt, g_col, cos_t, sin_t, rope):
    h = xt.shape[0] // GQA_HEAD_DIM
    x3 = xt.reshape(h, GQA_HEAD_DIM, xt.shape[1])
    ms = jnp.sum(x3 * x3, axis=1, keepdims=True) * (1.0 / GQA_HEAD_DIM)
    x3 = x3 * lax.rsqrt(ms + NORM_EPS) * g_col[None]
    if rope:
        swapped = jnp.concatenate([x3[:, 16:32], x3[:, 0:16], x3[:, 48:64], x3[:, 32:48]], axis=1)
        x3 = x3 * cos_t[None] + swapped * sin_t[None]
    return x3


def _in_proj_kernel(x_ref, mod_ref, g_ref, wn_ref, wt_ref, gw_ref, gb_ref, qg_ref, kg_ref,
                    cr_ref, sr_ref, ct_ref, st_ref,
                    rq_ref, rk_ref, rv_ref, rg_ref, gq_ref, gk_ref, gv_ref, gg_ref, gf_ref, gbw_ref,
                    aq_ref, ak_ref, av_ref, *, rope):
    x = x_ref[0]
    ms = jnp.mean(x * x, axis=-1, keepdims=True)
    h = x * lax.rsqrt(ms + NORM_EPS) * g_ref[...]
    h = h * (1.0 + mod_ref[0, 1:2, :]) + mod_ref[0, 0:1, :]
    hb = h.astype(BF16)
    p = _dot(hb, wn_ref[...])
    pt = _dot_nt(wt_ref[...], hb)

    rq = p[:, 0:128]
    rk = p[:, 128:256]
    gq = p[:, 768:896]
    if rope:
        cr = cr_ref[...]
        sr = sr_ref[...]
        rq = _rope_nat(rq, cr, sr, RET_DK // 4)
        rk = _rope_nat(rk, cr, sr, RET_DK // 4)
    rq_ref[0] = rq
    rk_ref[0] = rk * (RET_DK ** -0.5)
    rv_ref[0] = p[:, 256:512]
    rg_ref[0] = _silu(p[:, 512:768])
    gq_ref[0] = gq * (GLA_DK ** -0.5)
    gk_ref[0] = p[:, 896:1024]
    gv_ref[0] = p[:, 1024:1280]
    gg_ref[0] = _silu(p[:, 1280:1536])
    z = _dot(p[:, 1536:1664].astype(BF16), gw_ref[...]) + gb_ref[...]
    gates = _log_sigmoid(z) * (1.0 / GLA_TAU)
    gf_ref[0] = gates[:, 0:128]
    gbw_ref[0] = gates[:, 128:256]

    ct = ct_ref[...]
    st = st_ref[...]
    q3 = _head_norm_rope_t(pt[0:512], qg_ref[...], ct, st, rope)
    aq_ref[0] = (q3 * (GQA_HEAD_DIM ** -0.5 * LOG2_E)).astype(BF16)
    k3 = _head_norm_rope_t(pt[512:640], kg_ref[...], ct, st, rope)
    ak_ref[0] = k3.reshape(GQA_KV_HEADS * GQA_HEAD_DIM, k3.shape[2]).T.astype(BF16)
    av_ref[0] = pt[640:768].reshape(GQA_KV_HEADS, GQA_HEAD_DIM, pt.shape[1]).astype(BF16)


def _in_proj(x, mod, norm_g, w_nat, w_tr, gate_w_bd, gate_b_cat, q_gain, k_gain, tables, *, rope, tm):
    bsz, n, d = x.shape
    tm = min(tm, n)
    cos_r, sin_r, cos_t, sin_t = tables
    tok = lambda w: pl.BlockSpec((1, tm, w), lambda b, i: (b, i, 0))
    full = lambda a: pl.BlockSpec(a.shape, lambda b, i: (0,) * a.ndim)
    f32_out = lambda w: jax.ShapeDtypeStruct((bsz, n, w), F32)
    outs = pl.pallas_call(
        functools.partial(_in_proj_kernel, rope=rope),
        grid=(bsz, n // tm),
        in_specs=[tok(d),
                  pl.BlockSpec((1, 6, d), lambda b, i: (b, 0, 0)),
                  full(norm_g), full(w_nat), full(w_tr), full(gate_w_bd), full(gate_b_cat),
                  full(q_gain), full(k_gain),
                  pl.BlockSpec((tm, 128), lambda b, i: (i, 0)),
                  pl.BlockSpec((tm, 128), lambda b, i: (i, 0)),
                  pl.BlockSpec((GQA_HEAD_DIM, tm), lambda b, i: (0, i)),
                  pl.BlockSpec((GQA_HEAD_DIM, tm), lambda b, i: (0, i))],
        out_specs=[tok(128), tok(128), tok(256), tok(256), tok(128), tok(128), tok(256), tok(256),
                   tok(128), tok(128),
                   pl.BlockSpec((1, GQA_HEADS, GQA_HEAD_DIM, tm), lambda b, i: (b, 0, 0, i)),
                   tok(128),
                   pl.BlockSpec((1, GQA_KV_HEADS, GQA_HEAD_DIM, tm), lambda b, i: (b, 0, 0, i))],
        out_shape=[f32_out(128), f32_out(128), f32_out(256), f32_out(256),
                   f32_out(128), f32_out(128), f32_out(256), f32_out(256), f32_out(128), f32_out(128),
                   jax.ShapeDtypeStruct((bsz, GQA_HEADS, GQA_HEAD_DIM, n), BF16),
                   jax.ShapeDtypeStruct((bsz, n, 128), BF16),
                   jax.ShapeDtypeStruct((bsz, GQA_KV_HEADS, GQA_HEAD_DIM, n), BF16)],
        compiler_params=pltpu.CompilerParams(dimension_semantics=("parallel", "parallel"),
                                             vmem_limit_bytes=V7X_VMEM_LIMIT_BYTES),
        name="in_proj_rope" if rope else "in_proj_ctx",
    )(x, mod, norm_g, w_nat, w_tr, gate_w_bd, gate_b_cat, q_gain, k_gain, cos_r, sin_r, cos_t, sin_t)
    return outs


def _cumsum_rows(g, reverse):
    c = g.shape[0]
    row = lax.broadcasted_iota(jnp.int32, g.shape, 0)
    b = g
    s = 1
    while s < c:
        if reverse:
            b = b + jnp.where(row < c - s, pltpu.roll(b, c - s, 0), 0.0)
        else:
            b = b + jnp.where(row >= s, pltpu.roll(b, s, 0), 0.0)
        s *= 2
    return b


def _scan_chunk(q, k, v, b, b_end, st_ref, reverse):
    c = q.shape[0]
    qd = (q * jnp.exp(b)).astype(BF16)
    kd = (k * jnp.exp(-b)).astype(BF16)
    ku = (k * jnp.exp(b_end - b)).astype(BF16)
    vb = v.astype(BF16)

    lane_k = lax.broadcasted_iota(jnp.int32, (c, 128), 1) // RET_DK
    lane_v = lax.broadcasted_iota(jnp.int32, (c, 256), 1) // RET_DV
    zk = jnp.zeros_like(kd)
    zv = jnp.zeros_like(vb)
    kd_stack = jnp.concatenate([jnp.where(lane_k == h, kd, zk) for h in range(RET_HEADS)], axis=0)
    v_stack = jnp.concatenate([jnp.where(lane_v == h, vb, zv) for h in range(RET_HEADS)], axis=0)

    a = _dot_nt(qd, kd_stack)
    i = lax.broadcasted_iota(jnp.int32, a.shape, 0)
    j = lax.broadcasted_iota(jnp.int32, a.shape, 1) % c
    keep = (j > i) if reverse else (j <= i)
    a = jnp.where(keep, a, 0.0).astype(BF16)

    st = st_ref[...]
    o = _dot(a, v_stack) + _dot_nt(qd, st.astype(BF16))

    u = _dot_tn(vb, ku)
    r = lax.broadcasted_iota(jnp.int32, u.shape, 0) // RET_DV
    l = lax.broadcasted_iota(jnp.int32, u.shape, 1) // RET_DK
    st_ref[...] = st * jnp.exp(b_end) + jnp.where(r == l, u, 0.0)
    return o


def _scan_kernel(*refs, gated, tile):
    if gated:
        (qf_ref, kf_ref, vf_ref, gf_ref, qb_ref, kb_ref, vb_ref, gb_ref, of_ref, ob_ref, st_ref) = refs
    else:
        (qf_ref, kf_ref, vf_ref, qb_ref, kb_ref, vb_ref, lg_ref, of_ref, ob_ref, st_ref) = refs
    c = SCAN_CHUNK

    @pl.when(pl.program_id(1) == 0)
    def _():
        st_ref[...] = jnp.zeros_like(st_ref)

    n_chunks = tile // c
    for direction in range(2):
        reverse = direction == 1
        q_ref, k_ref, v_ref, o_ref = ((qb_ref, kb_ref, vb_ref, ob_ref) if reverse
                                      else (qf_ref, kf_ref, vf_ref, of_ref))
        order = range(n_chunks - 1, -1, -1) if reverse else range(n_chunks)
        for ci in order:
            rows = pl.ds(ci * c, c)
            if gated:
                g = (gb_ref if reverse else gf_ref)[0, rows, :]
                b = _cumsum_rows(g, reverse)
                b_end = b[0:1, :] if reverse else b[c - 1:c, :]
            else:
                lg = _log_sigmoid(lg_ref[direction:direction + 1, :])
                row = lax.broadcasted_iota(jnp.int32, (c, 128), 0)
                steps = (c - row) if reverse else (row + 1)
                b = steps.astype(F32) * lg
                b_end = float(c) * lg
            o_ref[0, rows, :] = _scan_chunk(q_ref[0, rows, :], k_ref[0, rows, :], v_ref[0, rows, :],
                                            b, b_end, st_ref.at[direction], reverse)


def _bidir_scan(q, k, v, gates, decay_logit, *, n_lat, tile):
    bsz, n, _ = q.shape
    n_tiles = n // tile
    n_lat_tiles = n_lat // tile
    assert n_tiles == n_lat_tiles + 1, "context must be exactly one scan tile"
    fwd = lambda b, s: (b, jnp.where(s == 0, n_lat_tiles, s - 1), 0)
    bwd = lambda b, s: (b, jnp.where(s == 0, n_lat_tiles, n_lat_tiles - s), 0)
    spec = lambda w, im: pl.BlockSpec((1, tile, w), im)
    gated = gates is not None
    if gated:
        args = (q, k, v, gates[0], q, k, v, gates[1])
        in_specs = [spec(128, fwd), spec(128, fwd), spec(256, fwd), spec(128, fwd),
                    spec(128, bwd), spec(128, bwd), spec(256, bwd), spec(128, bwd)]
    else:
        args = (q, k, v, q, k, v, decay_logit)
        in_specs = [spec(128, fwd), spec(128, fwd), spec(256, fwd),
                    spec(128, bwd), spec(128, bwd), spec(256, bwd),
                    pl.BlockSpec(decay_logit.shape, lambda b, s: (0, 0))]
    return pl.pallas_call(
        functools.partial(_scan_kernel, gated=gated, tile=tile),
        grid=(bsz, n_tiles),
        in_specs=in_specs,
        out_specs=[spec(256, fwd), spec(256, bwd)],
        out_shape=[jax.ShapeDtypeStruct((bsz, n, 256), F32)] * 2,
        scratch_shapes=[pltpu.VMEM((2, 256, 128), F32)],
        compiler_params=pltpu.CompilerParams(dimension_semantics=("parallel", "arbitrary"),
                                             vmem_limit_bytes=V7X_VMEM_LIMIT_BYTES),
        name="scan_gla" if gated else "scan_ret",
    )(*args)


def _attn_kernel(q_ref, k_ref, v_ref, o_ref, acc_ref, qpad_ref, *, n_kv_tiles):
    kv = pl.program_id(1)
    tq = q_ref.shape[3]
    tk = k_ref.shape[2]
    width = GQA_GROUP * tq
    qcat = jnp.concatenate([q_ref[0, g] for g in range(GQA_GROUP)], axis=1)
    zero = jnp.zeros_like(qcat)
    qpad_ref[...] = jnp.where(kv == 0, jnp.concatenate([qcat, zero], axis=0),
                              jnp.concatenate([zero, qcat], axis=0))
    acc_ref[...] = jnp.zeros_like(acc_ref)

    def body(j, m_all):
        n_cb = width // ATTN_COL_BLOCK
        m_parts = [m_all[:, cb * ATTN_COL_BLOCK:(cb + 1) * ATTN_COL_BLOCK] for cb in range(n_cb)]
        units = [(sub, cb) for sub in range(tk // ATTN_SUB_KEYS) for cb in range(n_cb)]
        rows = lambda sub: slice(sub * ATTN_SUB_KEYS, (sub + 1) * ATTN_SUB_KEYS)
        cols = lambda cb: slice(cb * ATTN_COL_BLOCK, (cb + 1) * ATTN_COL_BLOCK)

        def scores(u):
            sub, cb = units[u]
            return _dot(k_ref[0, j, rows(sub), :], qpad_ref[:, cols(cb)])

        pending = [scores(u) for u in range(min(ATTN_LOOKAHEAD, len(units)))]
        for u, (sub, cb) in enumerate(units):
            if u + ATTN_LOOKAHEAD < len(units):
                pending.append(scores(u + ATTN_LOOKAHEAD))
            s = pending.pop(0)
            m_new = jnp.maximum(m_parts[cb], jnp.max(s, axis=0, keepdims=True))
            alpha = jnp.exp2(m_parts[cb] - m_new)
            p = jnp.exp2(s - m_new).astype(BF16)
            acc_ref[:, cols(cb)] = acc_ref[:, cols(cb)] * alpha + _dot(v_ref[0, 0, j, :, rows(sub)], p)
            m_parts[cb] = m_new
        return jnp.concatenate(m_parts, axis=1)

    lax.fori_loop(0, n_kv_tiles, body, jnp.full((1, width), -jnp.inf, F32))

    acc = acc_ref[...]
    o = acc[0:GQA_HEAD_DIM] * (1.0 / acc[GQA_HEAD_DIM:GQA_HEAD_DIM + 1])
    o_ref[0] = jnp.concatenate([o[:, g * tq:(g + 1) * tq].T for g in range(GQA_GROUP)], axis=1).astype(o_ref.dtype)


def _attention(q_t, k_nat, v_t, *, tq, tk):
    bsz, _, _, nq = q_t.shape
    ns = k_nat.shape[1]
    tq = min(tq, nq)
    tk = min(tk, ns)
    n_kv_tiles = ns // tk
    k_tiles = k_nat.reshape(bsz, n_kv_tiles, tk, 128)
    ones = jnp.ones((bsz, GQA_KV_HEADS, V_ROWS - GQA_HEAD_DIM, ns), BF16)
    v_tiles = jnp.concatenate([v_t, ones], axis=2).reshape(bsz, GQA_KV_HEADS, V_ROWS, n_kv_tiles, tk)
    v_tiles = v_tiles.transpose(0, 1, 3, 2, 4)
    return pl.pallas_call(
        functools.partial(_attn_kernel, n_kv_tiles=n_kv_tiles),
        grid=(bsz, GQA_KV_HEADS, nq // tq),
        in_specs=[pl.BlockSpec((1, GQA_GROUP, GQA_HEAD_DIM, tq), lambda b, h, i: (b, h, 0, i)),
                  pl.BlockSpec((1, n_kv_tiles, tk, 128), lambda b, h, i: (b, 0, 0, 0)),
                  pl.BlockSpec((1, 1, n_kv_tiles, V_ROWS, tk), lambda b, h, i: (b, h, 0, 0, 0))],
        out_specs=pl.BlockSpec((1, tq, GQA_GROUP * GQA_HEAD_DIM), lambda b, h, i: (b, i, h)),
        out_shape=jax.ShapeDtypeStruct((bsz, nq, GQA_HEADS * GQA_HEAD_DIM), BF16),
        scratch_shapes=[pltpu.VMEM((V_ROWS, GQA_GROUP * tq), F32),
                        pltpu.VMEM((2 * GQA_HEAD_DIM, GQA_GROUP * tq), BF16)],
        compiler_params=pltpu.CompilerParams(dimension_semantics=("parallel", "parallel", "parallel"),
                                             vmem_limit_bytes=V7X_VMEM_LIMIT_BYTES),
        name="gqa_attention",
    )(q_t, k_tiles, v_tiles)


def _group_mean(x, gmat):
    hi = x.astype(BF16)
    lo = (x - hi.astype(F32)).astype(BF16)
    return (_dot(hi, gmat) + _dot(lo, gmat)) * (1.0 / RET_DV)


def _out_proj_kernel(x_ref, mod_ref, rf_ref, rb_ref, rg_ref, gf_ref, gb_ref, gg_ref, ya_ref, w_ref, o_ref):
    r = lax.broadcasted_iota(jnp.int32, (256, 256), 0) // RET_DV
    c = lax.broadcasted_iota(jnp.int32, (256, 256), 1) // RET_DV
    gmat = jnp.where(r == c, 1.0, 0.0).astype(BF16)

    ro = rf_ref[0] + rb_ref[0]
    rc = ro - _group_mean(ro, gmat)
    y_ret = rg_ref[0] * (rc * lax.rsqrt(_group_mean(rc * rc, gmat) + NORM_EPS))
    go = gf_ref[0] + gb_ref[0]
    y_gla = gg_ref[0] * (go * lax.rsqrt(_group_mean(go * go, gmat) + NORM_EPS))
    y = jnp.concatenate([y_ret.astype(BF16), y_gla.astype(BF16), ya_ref[0]], axis=1)
    o_ref[0] = x_ref[0] + mod_ref[0, 2:3, :] * _dot(y, w_ref[...])


def _out_proj(x, mod, scans, gates, y_attn, w_out, *, tm, tok_offset):
    bsz, n, d = x.shape
    tm = min(tm, n)
    tok = lambda w: pl.BlockSpec((1, tm, w), lambda b, i: (b, i, 0))
    off = lambda w: pl.BlockSpec((1, tm, w), lambda b, i: (b, i + tok_offset, 0))
    rf, rb, gf, gb = scans
    rg, gg = gates
    return pl.pallas_call(
        _out_proj_kernel,
        grid=(bsz, n // tm),
        in_specs=[tok(d), pl.BlockSpec((1, 6, d), lambda b, i: (b, 0, 0)),
                  off(256), off(256), tok(256), off(256), off(256), tok(256), tok(512),
                  pl.BlockSpec(w_out.shape, lambda b, i: (0, 0))],
        out_specs=tok(d),
        out_shape=jax.ShapeDtypeStruct((bsz, n, d), F32),
        compiler_params=pltpu.CompilerParams(dimension_semantics=("parallel", "parallel"),
                                             vmem_limit_bytes=V7X_VMEM_LIMIT_BYTES),
        name="out_proj",
    )(x, mod, rf, rb, rg, gf, gb, gg, y_attn, w_out)


def _mlp_kernel(x_ref, mod_ref, g_ref, w1_ref, w2_ref, fg_ref, o_ref, *, ff_tile, final_norm):
    x = x_ref[0]
    ms = jnp.mean(x * x, axis=-1, keepdims=True)
    h = x * lax.rsqrt(ms + NORM_EPS) * g_ref[...]
    hb = (h * (1.0 + mod_ref[0, 4:5, :]) + mod_ref[0, 3:4, :]).astype(BF16)
    d_ff = w1_ref.shape[1]
    acc = jnp.zeros(x.shape, F32)
    for c in range(d_ff // ff_tile):
        cols = slice(c * ff_tile, (c + 1) * ff_tile)
        a = jnp.maximum(_dot(hb, w1_ref[:, cols]), 0.0)
        acc = acc + _dot((a * a).astype(BF16), w2_ref[cols, :])
    y = x + mod_ref[0, 5:6, :] * acc
    if final_norm:
        y = y * lax.rsqrt(jnp.mean(y * y, axis=-1, keepdims=True) + NORM_EPS) * fg_ref[...]
    o_ref[0] = y


def _mlp(x, mod, norm_g, w1, w2, final_g, *, tm, final_norm):
    bsz, n, d = x.shape
    tm = min(tm, n)
    tok = pl.BlockSpec((1, tm, d), lambda b, i: (b, i, 0))
    full = lambda a: pl.BlockSpec(a.shape, lambda b, i: (0,) * a.ndim, pipeline_mode=pl.Buffered(1))
    return pl.pallas_call(
        functools.partial(_mlp_kernel, ff_tile=1024, final_norm=final_norm),
        grid=(bsz, n // tm),
        in_specs=[tok, pl.BlockSpec((1, 6, d), lambda b, i: (b, 0, 0)),
                  full(norm_g), full(w1), full(w2), full(final_g)],
        out_specs=tok,
        out_shape=jax.ShapeDtypeStruct((bsz, n, d), F32),
        compiler_params=pltpu.CompilerParams(dimension_semantics=("parallel", "parallel"),
                                             vmem_limit_bytes=V7X_VMEM_LIMIT_BYTES),
        name="mlp_final" if final_norm else "mlp",
    )(x, mod, norm_g, w1, w2, final_g)


def _prep_w_in(w):
    d = w.shape[0]
    ret = w[:, 0:768]
    gla = w[:, 768:1536]
    low = w[:, 1536:1568]
    aq = w[:, 1568:2080]
    ak = w[:, 2080:2208]
    av = w[:, 2208:2336]
    w_nat = jnp.concatenate([ret, gla, low, jnp.zeros((d, 128 - 2 * GLA_RANK), w.dtype)], axis=1)
    w_tr = jnp.concatenate([aq, ak, av], axis=1).T
    return w_nat.astype(BF16), w_tr.astype(BF16)


def _rope_tables(n_lat):
    t = jnp.arange(n_lat)
    row = (t // GRID_W).astype(F32)
    col = (t % GRID_W).astype(F32)

    def head_tables(head_dim):
        nf = head_dim // 4
        inv_freq = ROPE_BASE ** (-jnp.arange(nf, dtype=F32) / nf)
        ang_r = row[:, None] * inv_freq
        ang_c = col[:, None] * inv_freq
        cos = jnp.concatenate([jnp.cos(ang_r)] * 2 + [jnp.cos(ang_c)] * 2, axis=1)
        sin = jnp.concatenate([-jnp.sin(ang_r), jnp.sin(ang_r), -jnp.sin(ang_c), jnp.sin(ang_c)], axis=1)
        return cos, sin

    cos_r, sin_r = head_tables(RET_DK)
    cos_r = jnp.tile(cos_r, (1, RET_HEADS))
    sin_r = jnp.tile(sin_r, (1, RET_HEADS))
    cos_a, sin_a = head_tables(GQA_HEAD_DIM)
    return cos_r, sin_r, cos_a.T, sin_a.T


def kernel(x, c, ctx, c_ctx, mod_w, mod_b, attn_norm_g, mlp_norm_g, w_in, w_out, ret_decay_logit,
           gla_gate_w, gla_gate_b, qk_norm_g, mlp_w1, mlp_w2, final_norm_g):
    bsz, n_lat, d = x.shape
    n_ctx = ctx.shape[1]
    depth = mod_w.shape[0]
    scan_tile = n_ctx
    assert n_lat % scan_tile == 0 and scan_tile % SCAN_CHUNK == 0 and bsz <= 7

    cvec = jnp.zeros((8, d), F32).at[:bsz].set(c).at[bsz].set(c_ctx)
    mods = _modulation(cvec, mod_w, mod_b)
    lat_tables = _rope_tables(n_lat)
    ctx_tables = tuple(jnp.zeros((n_ctx, 128), F32) for _ in range(2)) + \
        tuple(jnp.zeros((GQA_HEAD_DIM, n_ctx), F32) for _ in range(2))
    final_g = final_norm_g.reshape(1, d)

    for i in range(depth):
        need_ctx = i < depth - 1
        mod_lat = mods[i, :bsz].reshape(bsz, 6, d)
        mod_ctx = jnp.broadcast_to(mods[i, bsz].reshape(1, 6, d), (bsz, 6, d))
        w_nat, w_tr = _prep_w_in(w_in[i])
        gw = jnp.zeros((128, 256), F32)
        gw = gw.at[0:GLA_RANK, 0:128].set(gla_gate_w[i, 0]).at[GLA_RANK:2 * GLA_RANK, 128:256].set(gla_gate_w[i, 1])
        gw = gw.astype(BF16)
        gb = gla_gate_b[i].reshape(1, 256)
        q_gain = qk_norm_g[i, 0].reshape(GQA_HEAD_DIM, 1)
        k_gain = qk_norm_g[i, 1].reshape(GQA_HEAD_DIM, 1)
        attn_g = attn_norm_g[i].reshape(1, d)
        mlp_g = mlp_norm_g[i].reshape(1, d)
        decay = jnp.repeat(ret_decay_logit[i], RET_DK, axis=1)
        w_o = w_out[i].astype(BF16)
        w1 = mlp_w1[i].astype(BF16)
        w2 = mlp_w2[i].astype(BF16)

        pl_lat = _in_proj(x, mod_lat, attn_g, w_nat, w_tr, gw, gb, q_gain, k_gain, lat_tables, rope=True, tm=512)
        pl_ctx = _in_proj(ctx, mod_ctx, attn_g, w_nat, w_tr, gw, gb, q_gain, k_gain, ctx_tables, rope=False,
                          tm=n_ctx)
        cat = lambda j, axis=1: jnp.concatenate([pl_lat[j], pl_ctx[j]], axis=axis)
        rq, rk, rv = cat(0), cat(1), cat(2)
        gq, gk, gv, gf, gbw = cat(4), cat(5), cat(6), cat(8), cat(9)
        ak = cat(11)
        av = cat(12, axis=3)

        ret_f, ret_b = _bidir_scan(rq, rk, rv, None, decay, n_lat=n_lat, tile=scan_tile)
        gla_f, gla_b = _bidir_scan(gq, gk, gv, (gf, gbw), None, n_lat=n_lat, tile=scan_tile)
        y_attn = _attention(pl_lat[10], ak, av, tq=256, tk=1280)

        scans = (ret_f, ret_b, gla_f, gla_b)
        x = _out_proj(x, mod_lat, scans, (pl_lat[3], pl_lat[7]), y_attn, w_o, tm=scan_tile, tok_offset=0)
        x = _mlp(x, mod_lat, mlp_g, w1, w2, final_g, tm=512, final_norm=not need_ctx)
        if need_ctx:
            y_attn_c = _attention(pl_ctx[10], pl_ctx[11], pl_ctx[12], tq=256, tk=256)
            ctx = _out_proj(ctx, mod_ctx, scans, (pl_ctx[3], pl_ctx[7]), y_attn_c, w_o, tm=scan_tile,
                            tok_offset=n_lat // scan_tile)
            ctx = _mlp(ctx, mod_ctx, mlp_g, w1, w2, final_g, tm=n_ctx, final_norm=False)
    return x
```

```python
import functools

import jax
import jax.numpy as jnp
import numpy as np
from jax import lax
from jax.experimental import pallas as pl
from jax.experimental.pallas import tpu as pltpu

F32 = jnp.float32
BF16 = jnp.bfloat16

GRID_W = 64
RET_HEADS = 4
RET_DK = 32
RET_DV = 64
GLA_HEADS = 4
GLA_DK = 32
GLA_DV = 64
GLA_RANK = 16
GLA_TAU = 16.0
GQA_HEADS = 8
GQA_KV_HEADS = 2
GQA_GROUP = 4
GQA_HEAD_DIM = 64
SCAN_CHUNK = 64
ROPE_BASE = 10000.0
NORM_EPS = 1e-6

V7X_VMEM_LIMIT_BYTES = 56 * 1024 * 1024
V_ROWS = 80
ATTN_SUB_KEYS = 128
ATTN_COL_BLOCK = 256
ATTN_LOOKAHEAD = 12
LOG2_E = 1.4426950408889634

NAT_WIDTH = 1664
TR_WIDTH = 768


def _silu(x):
    return x * (1.0 / (1.0 + jnp.exp(-x)))


def _log_sigmoid(z):
    return jnp.minimum(z, 0.0) - jnp.log(1.0 + jnp.exp(-jnp.abs(z)))


def _dot(a, b):
    return jnp.dot(a, b, preferred_element_type=F32)


def _dot_nt(a, b):
    return lax.dot_general(a, b, (((1,), (1,)), ((), ())), preferred_element_type=F32)


def _dot_tn(a, b):
    return lax.dot_general(a, b, (((0,), (0,)), ((), ())), preferred_element_type=F32)


def _mod_kernel(c_ref, w_ref, b_ref, o_ref):
    s = _silu(c_ref[...]).astype(BF16)
    o_ref[0] = _dot(s, w_ref[0].astype(BF16)) + b_ref[0]


def _modulation(cvec, mod_w, mod_b):
    depth, d, n = mod_w.shape
    tn = 1536
    return pl.pallas_call(
        _mod_kernel,
        grid=(depth, n // tn),
        in_specs=[pl.BlockSpec((8, d), lambda l, j: (0, 0)),
                  pl.BlockSpec((1, d, tn), lambda l, j: (l, 0, j)),
                  pl.BlockSpec((1, 1, tn), lambda l, j: (l, 0, j))],
        out_specs=pl.BlockSpec((1, 8, tn), lambda l, j: (l, 0, j)),
        out_shape=jax.ShapeDtypeStruct((depth, 8, n), F32),
        compiler_params=pltpu.CompilerParams(dimension_semantics=("parallel", "parallel"),
                                             vmem_limit_bytes=V7X_VMEM_LIMIT_BYTES),
        name="modulation",
    )(cvec, mod_w, mod_b.reshape(depth, 1, n))


def _rope_nat(x, cos, sin_signed, nf):
    lane = lax.broadcasted_iota(jnp.int32, x.shape, 1)
    first = (lane % (2 * nf)) < nf
    partner = jnp.where(first, pltpu.roll(x, 128 - nf, 1), pltpu.roll(x, nf, 1))
    return x * cos + partner * sin_signed


def _head_norm_rope_t(xt, g_col, cos_t, sin_t, rope):
    h = xt.shape[0] // GQA_HEAD_DIM
    x3 = xt.reshape(h, GQA_HEAD_DIM, xt.shape[1])
    ms = jnp.sum(x3 * x3, axis=1, keepdims=True) * (1.0 / GQA_HEAD_DIM)
    x3 = x3 * lax.rsqrt(ms + NORM_EPS) * g_col[None]
    if rope:
        swapped = jnp.concatenate([x3[:, 16:32], x3[:, 0:16], x3[:, 48:64], x3[:, 32:48]], axis=1)
        x3 = x3 * cos_t[None] + swapped * sin_t[None]
    return x3


def _in_proj_kernel(x_ref, mod_ref, g_ref, wn_ref, wt_ref, gw_ref, gb_ref, qg_ref, kg_ref,
                    cr_ref, sr_ref, ct_ref, st_ref,
                    rq_ref, rk_ref, rv_ref, rg_ref, gq_ref, gk_ref, gv_ref, gg_ref, gf_ref, gbw_ref,
                    aq_ref, ak_ref, av_ref, *, rope):
    x = x_ref[0]
    ms = jnp.mean(x * x, axis=-1, keepdims=True)
    h = x * lax.rsqrt(ms + NORM_EPS) * g_ref[...]
    h = h * (1.0 + mod_ref[0, 1:2, :]) + mod_ref[0, 0:1, :]
    hb = h.astype(BF16)
    p = _dot(hb, wn_ref[...])
    pt = _dot_nt(wt_ref[...], hb)

    rq = p[:, 0:128]
    rk = p[:, 128:256]
    gq = p[:, 768:896]
    if rope:
        cr = cr_ref[...]
        sr = sr_ref[...]
        rq = _rope_nat(rq, cr, sr, RET_DK // 4)
        rk = _rope_nat(rk, cr, sr, RET_DK // 4)
    rq_ref[0] = rq
    rk_ref[0] = rk * (RET_DK ** -0.5)
    rv_ref[0] = p[:, 256:512]
    rg_ref[0] = _silu(p[:, 512:768])
    gq_ref[0] = gq * (GLA_DK ** -0.5)
    gk_ref[0] = p[:, 896:1024]
    gv_ref[0] = p[:, 1024:1280]
    gg_ref[0] = _silu(p[:, 1280:1536])
    z = _dot(p[:, 1536:1664].astype(BF16), gw_ref[...]) + gb_ref[...]
    gates = _log_sigmoid(z) * (1.0 / GLA_TAU)
    gf_ref[0] = gates[:, 0:128]
    gbw_ref[0] = gates[:, 128:256]

    ct = ct_ref[...]
    st = st_ref[...]
    q3 = _head_norm_rope_t(pt[0:512], qg_ref[...], ct, st, rope)
    aq_ref[0] = (q3 * (GQA_HEAD_DIM ** -0.5 * LOG2_E)).astype(BF16)
    k3 = _head_norm_rope_t(pt[512:640], kg_ref[...], ct, st, rope)
    ak_ref[0] = k3.reshape(GQA_KV_HEADS * GQA_HEAD_DIM, k3.shape[2]).T.astype(BF16)
    av_ref[0] = pt[640:768].reshape(GQA_KV_HEADS, GQA_HEAD_DIM, pt.shape[1]).astype(BF16)


def _in_proj(x, mod, norm_g, w_nat, w_tr, gate_w_bd, gate_b_cat, q_gain, k_gain, tables, *, rope, tm):
    bsz, n, d = x.shape
    tm = min(tm, n)
    cos_r, sin_r, cos_t, sin_t = tables
    tok = lambda w: pl.BlockSpec((1, tm, w), lambda b, i: (b, i, 0))
    full = lambda a: pl.BlockSpec(a.shape, lambda b, i: (0,) * a.ndim)
    f32_out = lambda w: jax.ShapeDtypeStruct((bsz, n, w), F32)
    outs = pl.pallas_call(
        functools.partial(_in_proj_kernel, rope=rope),
        grid=(bsz, n // tm),
        in_specs=[tok(d),
                  pl.BlockSpec((1, 6, d), lambda b, i: (b, 0, 0)),
                  full(norm_g), full(w_nat), full(w_tr), full(gate_w_bd), full(gate_b_cat),
                  full(q_gain), full(k_gain),
                  pl.BlockSpec((tm, 128), lambda b, i: (i, 0)),
                  pl.BlockSpec((tm, 128), lambda b, i: (i, 0)),
                  pl.BlockSpec((GQA_HEAD_DIM, tm), lambda b, i: (0, i)),
                  pl.BlockSpec((GQA_HEAD_DIM, tm), lambda b, i: (0, i))],
        out_specs=[tok(128), tok(128), tok(256), tok(256), tok(128), tok(128), tok(256), tok(256),
                   tok(128), tok(128),
                   pl.BlockSpec((1, GQA_HEADS, GQA_HEAD_DIM, tm), lambda b, i: (b, 0, 0, i)),
                   tok(128),
                   pl.BlockSpec((1, GQA_KV_HEADS, GQA_HEAD_DIM, tm), lambda b, i: (b, 0, 0, i))],
        out_shape=[f32_out(128), f32_out(128), f32_out(256), f32_out(256),
                   f32_out(128), f32_out(128), f32_out(256), f32_out(256), f32_out(128), f32_out(128),
                   jax.ShapeDtypeStruct((bsz, GQA_HEADS, GQA_HEAD_DIM, n), BF16),
                   jax.ShapeDtypeStruct((bsz, n, 128), BF16),
                   jax.ShapeDtypeStruct((bsz, GQA_KV_HEADS, GQA_HEAD_DIM, n), BF16)],
        compiler_params=pltpu.CompilerParams(dimension_semantics=("parallel", "parallel"),
                                             vmem_limit_bytes=V7X_VMEM_LIMIT_BYTES),
        name="in_proj_rope" if rope else "in_proj_ctx",
    )(x, mod, norm_g, w_nat, w_tr, gate_w_bd, gate_b_cat, q_gain, k_gain, cos_r, sin_r, cos_t, sin_t)
    return outs


def _cumsum_rows(g, reverse):
    c = g.shape[0]
    row = lax.broadcasted_iota(jnp.int32, g.shape, 0)
    b = g
    s = 1
    while s < c:
        if reverse:
            b = b + jnp.where(row < c - s, pltpu.roll(b, c - s, 0), 0.0)
        else:
            b = b + jnp.where(row >= s, pltpu.roll(b, s, 0), 0.0)
        s *= 2
    return b


def _scan_chunk(q, k, v, b, b_end, st_ref, reverse):
    c = q.shape[0]
    qd = (q * jnp.exp(b)).astype(BF16)
    kd = (k * jnp.exp(-b)).astype(BF16)
    ku = (k * jnp.exp(b_end - b)).astype(BF16)
    vb = v.astype(BF16)

    lane_k = lax.broadcasted_iota(jnp.int32, (c, 128), 1) // RET_DK
    lane_v = lax.broadcasted_iota(jnp.int32, (c, 256), 1) // RET_DV
    zk = jnp.zeros_like(kd)
    zv = jnp.zeros_like(vb)
    kd_stack = jnp.concatenate([jnp.where(lane_k == h, kd, zk) for h in range(RET_HEADS)], axis=0)
    v_stack = jnp.concatenate([jnp.where(lane_v == h, vb, zv) for h in range(RET_HEADS)], axis=0)

    a = _dot_nt(qd, kd_stack)
    i = lax.broadcasted_iota(jnp.int32, a.shape, 0)
    j = lax.broadcasted_iota(jnp.int32, a.shape, 1) % c
    keep = (j > i) if reverse else (j <= i)
    a = jnp.where(keep, a, 0.0).astype(BF16)

    st = st_ref[...]
    o = _dot(a, v_stack) + _dot_nt(qd, st.astype(BF16))

    u = _dot_tn(vb, ku)
    r = lax.broadcasted_iota(jnp.int32, u.shape, 0) // RET_DV
    l = lax.broadcasted_iota(jnp.int32, u.shape, 1) // RET_DK
    st_ref[...] = st * jnp.exp(b_end) + jnp.where(r == l, u, 0.0)
    return o


def _scan_kernel(*refs, gated, tile):
    if gated:
        (qf_ref, kf_ref, vf_ref, gf_ref, qb_ref, kb_ref, vb_ref, gb_ref, of_ref, ob_ref, st_ref) = refs
    else:
        (qf_ref, kf_ref, vf_ref, qb_ref, kb_ref, vb_ref, lg_ref, of_ref, ob_ref, st_ref) = refs
    c = SCAN_CHUNK

    @pl.when(pl.program_id(1) == 0)
    def _():
        st_ref[...] = jnp.zeros_like(st_ref)

    n_chunks = tile // c
    for direction in range(2):
        reverse = direction == 1
        q_ref, k_ref, v_ref, o_ref = ((qb_ref, kb_ref, vb_ref, ob_ref) if reverse
                                      else (qf_ref, kf_ref, vf_ref, of_ref))
        order = range(n_chunks - 1, -1, -1) if reverse else range(n_chunks)
        for ci in order:
            rows = pl.ds(ci * c, c)
            if gated:
                g = (gb_ref if reverse else gf_ref)[0, rows, :]
                b = _cumsum_rows(g, reverse)
                b_end = b[0:1, :] if reverse else b[c - 1:c, :]
            else:
                lg = _log_sigmoid(lg_ref[direction:direction + 1, :])
                row = lax.broadcasted_iota(jnp.int32, (c, 128), 0)
                steps = (c - row) if reverse else (row + 1)
                b = steps.astype(F32) * lg
                b_end = float(c) * lg
            o_ref[0, rows, :] = _scan_chunk(q_ref[0, rows, :], k_ref[0, rows, :], v_ref[0, rows, :],
                                            b, b_end, st_ref.at[direction], reverse)


def _bidir_scan(q, k, v, gates, decay_logit, *, n_lat, tile):
    bsz, n, _ = q.shape
    n_tiles = n // tile
    n_lat_tiles = n_lat // tile
    assert n_tiles == n_lat_tiles + 1, "context must be exactly one scan tile"
    fwd = lambda b, s: (b, jnp.where(s == 0, n_lat_tiles, s - 1), 0)
    bwd = lambda b, s: (b, jnp.where(s == 0, n_lat_tiles, n_lat_tiles - s), 0)
    spec = lambda w, im: pl.BlockSpec((1, tile, w), im)
    gated = gates is not None
    if gated:
        args = (q, k, v, gates[0], q, k, v, gates[1])
        in_specs = [spec(128, fwd), spec(128, fwd), spec(256, fwd), spec(128, fwd),
                    spec(128, bwd), spec(128, bwd), spec(256, bwd), spec(128, bwd)]
    else:
        args = (q, k, v, q, k, v, decay_logit)
        in_specs = [spec(128, fwd), spec(128, fwd), spec(256, fwd),
                    spec(128, bwd), spec(128, bwd), spec(256, bwd),
                    pl.BlockSpec(decay_logit.shape, lambda b, s: (0, 0))]
    return pl.pallas_call(
        functools.partial(_scan_kernel, gated=gated, tile=tile),
        grid=(bsz, n_tiles),
        in_specs=in_specs,
        out_specs=[spec(256, fwd), spec(256, bwd)],
        out_shape=[jax.ShapeDtypeStruct((bsz, n, 256), F32)] * 2,
        scratch_shapes=[pltpu.VMEM((2, 256, 128), F32)],
        compiler_params=pltpu.CompilerParams(dimension_semantics=("parallel", "arbitrary"),
                                             vmem_limit_bytes=V7X_VMEM_LIMIT_BYTES),
        name="scan_gla" if gated else "scan_ret",
    )(*args)


def _attn_kernel(q_ref, k_ref, v_ref, o_ref, acc_ref, qpad_ref, *, n_kv_tiles):
    kv = pl.program_id(1)
    tq = q_ref.shape[3]
    tk = k_ref.shape[2]
    width = GQA_GROUP * tq
    qcat = jnp.concatenate([q_ref[0, g] for g in range(GQA_GROUP)], axis=1)
    zero = jnp.zeros_like(qcat)
    qpad_ref[...] = jnp.where(kv == 0, jnp.concatenate([qcat, zero], axis=0),
                              jnp.concatenate([zero, qcat], axis=0))
    acc_ref[...] = jnp.zeros_like(acc_ref)

    def body(j, m_all):
        n_cb = width // ATTN_COL_BLOCK
        m_parts = [m_all[:, cb * ATTN_COL_BLOCK:(cb + 1) * ATTN_COL_BLOCK] for cb in range(n_cb)]
        units = [(sub, cb) for sub in range(tk // ATTN_SUB_KEYS) for cb in range(n_cb)]
        rows = lambda sub: slice(sub * ATTN_SUB_KEYS, (sub + 1) * ATTN_SUB_KEYS)
        cols = lambda cb: slice(cb * ATTN_COL_BLOCK, (cb + 1) * ATTN_COL_BLOCK)

        def scores(u):
            sub, cb = units[u]
            return _dot(k_ref[0, j, rows(sub), :], qpad_ref[:, cols(cb)])

        pending = [scores(u) for u in range(min(ATTN_LOOKAHEAD, len(units)))]
        for u, (sub, cb) in enumerate(units):
            if u + ATTN_LOOKAHEAD < len(units):
                pending.append(scores(u + ATTN_LOOKAHEAD))
            s = pending.pop(0)
            m_new = jnp.maximum(m_parts[cb], jnp.max(s, axis=0, keepdims=True))
            alpha = jnp.exp2(m_parts[cb] - m_new)
            p = jnp.exp2(s - m_new).astype(BF16)
            acc_ref[:, cols(cb)] = acc_ref[:, cols(cb)] * alpha + _dot(v_ref[0, 0, j, :, rows(sub)], p)
            m_parts[cb] = m_new
        return jnp.concatenate(m_parts, axis=1)

    lax.fori_loop(0, n_kv_tiles, body, jnp.full((1, width), -jnp.inf, F32))

    acc = acc_ref[...]
    o = acc[0:GQA_HEAD_DIM] * (1.0 / acc[GQA_HEAD_DIM:GQA_HEAD_DIM + 1])
    o_ref[0] = jnp.concatenate([o[:, g * tq:(g + 1) * tq].T for g in range(GQA_GROUP)], axis=1).astype(o_ref.dtype)


def _attention(q_t, k_nat, v_t, *, tq, tk):
    bsz, _, _, nq = q_t.shape
    ns = k_nat.shape[1]
    tq = min(tq, nq)
    tk = min(tk, ns)
    n_kv_tiles = ns // tk
    k_tiles = k_nat.reshape(bsz, n_kv_tiles, tk, 128)
    ones = jnp.ones((bsz, GQA_KV_HEADS, V_ROWS - GQA_HEAD_DIM, ns), BF16)
    v_tiles = jnp.concatenate([v_t, ones], axis=2).reshape(bsz, GQA_KV_HEADS, V_ROWS, n_kv_tiles, tk)
    v_tiles = v_tiles.transpose(0, 1, 3, 2, 4)
    return pl.pallas_call(
        functools.partial(_attn_kernel, n_kv_tiles=n_kv_tiles),
        grid=(bsz, GQA_KV_HEADS, nq // tq),
        in_specs=[pl.BlockSpec((1, GQA_GROUP, GQA_HEAD_DIM, tq), lambda b, h, i: (b, h, 0, i)),
                  pl.BlockSpec((1, n_kv_tiles, tk, 128), lambda b, h, i: (b, 0, 0, 0)),
                  pl.BlockSpec((1, 1, n_kv_tiles, V_ROWS, tk), lambda b, h, i: (b, h, 0, 0, 0))],
        out_specs=pl.BlockSpec((1, tq, GQA_GROUP * GQA_HEAD_DIM), lambda b, h, i: (b, i, h)),
        out_shape=jax.ShapeDtypeStruct((bsz, nq, GQA_HEADS * GQA_HEAD_DIM), BF16),
        scratch_shapes=[pltpu.VMEM((V_ROWS, GQA_GROUP * tq), F32),
                        pltpu.VMEM((2 * GQA_HEAD_DIM, GQA_GROUP * tq), BF16)],
        compiler_params=pltpu.CompilerParams(dimension_semantics=("parallel", "parallel", "parallel"),
                                             vmem_limit_bytes=V7X_VMEM_LIMIT_BYTES),
        name="gqa_attention",
    )(q_t, k_tiles, v_tiles)


def _group_mean(x, gmat):
    hi = x.astype(BF16)
    lo = (x - hi.astype(F32)).astype(BF16)
    return (_dot(hi, gmat) + _dot(lo, gmat)) * (1.0 / RET_DV)


def _out_proj_kernel(x_ref, mod_ref, rf_ref, rb_ref, rg_ref, gf_ref, gb_ref, gg_ref, ya_ref, w_ref, o_ref):
    r = lax.broadcasted_iota(jnp.int32, (256, 256), 0) // RET_DV
    c = lax.broadcasted_iota(jnp.int32, (256, 256), 1) // RET_DV
    gmat = jnp.where(r == c, 1.0, 0.0).astype(BF16)

    ro = rf_ref[0] + rb_ref[0]
    rc = ro - _group_mean(ro, gmat)
    y_ret = rg_ref[0] * (rc * lax.rsqrt(_group_mean(rc * rc, gmat) + NORM_EPS))
    go = gf_ref[0] + gb_ref[0]
    y_gla = gg_ref[0] * (go * lax.rsqrt(_group_mean(go * go, gmat) + NORM_EPS))
    y = jnp.concatenate([y_ret.astype(BF16), y_gla.astype(BF16), ya_ref[0]], axis=1)
    o_ref[0] = x_ref[0] + mod_ref[0, 2:3, :] * _dot(y, w_ref[...])


def _out_proj(x, mod, scans, gates, y_attn, w_out, *, tm, tok_offset):
    bsz, n, d = x.shape
    tm = min(tm, n)
    tok = lambda w: pl.BlockSpec((1, tm, w), lambda b, i: (b, i, 0))
    off = lambda w: pl.BlockSpec((1, tm, w), lambda b, i: (b, i + tok_offset, 0))
    rf, rb, gf, gb = scans
    rg, gg = gates
    return pl.pallas_call(
        _out_proj_kernel,
        grid=(bsz, n // tm),
        in_specs=[tok(d), pl.BlockSpec((1, 6, d), lambda b, i: (b, 0, 0)),
                  off(256), off(256), tok(256), off(256), off(256), tok(256), tok(512),
                  pl.BlockSpec(w_out.shape, lambda b, i: (0, 0))],
        out_specs=tok(d),
        out_shape=jax.ShapeDtypeStruct((bsz, n, d), F32),
        compiler_params=pltpu.CompilerParams(dimension_semantics=("parallel", "parallel"),
                                             vmem_limit_bytes=V7X_VMEM_LIMIT_BYTES),
        name="out_proj",
    )(x, mod, rf, rb, rg, gf, gb, gg, y_attn, w_out)


def _mlp_kernel(x_ref, mod_ref, g_ref, w1_ref, w2_ref, fg_ref, o_ref, *, ff_tile, final_norm):
    x = x_ref[0]
    ms = jnp.mean(x * x, axis=-1, keepdims=True)
    h = x * lax.rsqrt(ms + NORM_EPS) * g_ref[...]
    hb = (h * (1.0 + mod_ref[0, 4:5, :]) + mod_ref[0, 3:4, :]).astype(BF16)
    d_ff = w1_ref.shape[1]
    acc = jnp.zeros(x.shape, F32)
    for c in range(d_ff // ff_tile):
        cols = slice(c * ff_tile, (c + 1) * ff_tile)
        a = jnp.maximum(_dot(hb, w1_ref[:, cols]), 0.0)
        acc = acc + _dot((a * a).astype(BF16), w2_ref[cols, :])
    y = x + mod_ref[0, 5:6, :] * acc
    if final_norm:
        y = y * lax.rsqrt(jnp.mean(y * y, axis=-1, keepdims=True) + NORM_EPS) * fg_ref[...]
    o_ref[0] = y


def _mlp(x, mod, norm_g, w1, w2, final_g, *, tm, final_norm):
    bsz, n, d = x.shape
    tm = min(tm, n)
    tok = pl.BlockSpec((1, tm, d), lambda b, i: (b, i, 0))
    full = lambda a: pl.BlockSpec(a.shape, lambda b, i: (0,) * a.ndim, pipeline_mode=pl.Buffered(1))
    return pl.pallas_call(
        functools.partial(_mlp_kernel, ff_tile=1024, final_norm=final_norm),
        grid=(bsz, n // tm),
        in_specs=[tok, pl.BlockSpec((1, 6, d), lambda b, i: (b, 0, 0)),
                  full(norm_g), full(w1), full(w2), full(final_g)],
        out_specs=tok,
        out_shape=jax.ShapeDtypeStruct((bsz, n, d), F32),
        compiler_params=pltpu.CompilerParams(dimension_semantics=("parallel", "parallel"),
                                             vmem_limit_bytes=V7X_VMEM_LIMIT_BYTES),
        name="mlp_final" if final_norm else "mlp",
    )(x, mod, norm_g, w1, w2, final_g)


def _prep_w_in(w):
    d = w.shape[0]
    ret = w[:, 0:768]
    gla = w[:, 768:1536]
    low = w[:, 1536:1568]
    aq = w[:, 1568:2080]
    ak = w[:, 2080:2208]
    av = w[:, 2208:2336]
    w_nat = jnp.concatenate([ret, gla, low, jnp.zeros((d, 128 - 2 * GLA_RANK), w.dtype)], axis=1)
    w_tr = jnp.concatenate([aq, ak, av], axis=1).T
    return w_nat.astype(BF16), w_tr.astype(BF16)


def _rope_tables(n_lat):
    t = jnp.arange(n_lat)
    row = (t // GRID_W).astype(F32)
    col = (t % GRID_W).astype(F32)

    def head_tables(head_dim):
        nf = head_dim // 4
        inv_freq = ROPE_BASE ** (-jnp.arange(nf, dtype=F32) / nf)
        ang_r = row[:, None] * inv_freq
        ang_c = col[:, None] * inv_freq
        cos = jnp.concatenate([jnp.cos(ang_r)] * 2 + [jnp.cos(ang_c)] * 2, axis=1)
        sin = jnp.concatenate([-jnp.sin(ang_r), jnp.sin(ang_r), -jnp.sin(ang_c), jnp.sin(ang_c)], axis=1)
        return cos, sin

    cos_r, sin_r = head_tables(RET_DK)
    cos_r = jnp.tile(cos_r, (1, RET_HEADS))
    sin_r = jnp.tile(sin_r, (1, RET_HEADS))
    cos_a, sin_a = head_tables(GQA_HEAD_DIM)
    return cos_r, sin_r, cos_a.T, sin_a.T


def kernel(x, c, ctx, c_ctx, mod_w, mod_b, attn_norm_g, mlp_norm_g, w_in, w_out, ret_decay_logit,
           gla_gate_w, gla_gate_b, qk_norm_g, mlp_w1, mlp_w2, final_norm_g):
    bsz, n_lat, d = x.shape
    n_ctx = ctx.shape[1]
    depth = mod_w.shape[0]
    scan_tile = n_ctx
    assert n_lat % scan_tile == 0 and scan_tile % SCAN_CHUNK == 0 and bsz <= 7

    cvec = jnp.zeros((8, d), F32).at[:bsz].set(c).at[bsz].set(c_ctx)
    mods = _modulation(cvec, mod_w, mod_b)
    lat_tables = _rope_tables(n_lat)
    ctx_tables = tuple(jnp.zeros((n_ctx, 128), F32) for _ in range(2)) + \
        tuple(jnp.zeros((GQA_HEAD_DIM, n_ctx), F32) for _ in range(2))
    final_g = final_norm_g.reshape(1, d)

    for i in range(depth):
        need_ctx = i < depth - 1
        mod_lat = mods[i, :bsz].reshape(bsz, 6, d)
        mod_ctx = jnp.broadcast_to(mods[i, bsz].reshape(1, 6, d), (bsz, 6, d))
        w_nat, w_tr = _prep_w_in(w_in[i])
        gw = jnp.zeros((128, 256), F32)
        gw = gw.at[0:GLA_RANK, 0:128].set(gla_gate_w[i, 0]).at[GLA_RANK:2 * GLA_RANK, 128:256].set(gla_gate_w[i, 1])
        gw = gw.astype(BF16)
        gb = gla_gate_b[i].reshape(1, 256)
        q_gain = qk_norm_g[i, 0].reshape(GQA_HEAD_DIM, 1)
        k_gain = qk_norm_g[i, 1].reshape(GQA_HEAD_DIM, 1)
        attn_g = attn_norm_g[i].reshape(1, d)
        mlp_g = mlp_norm_g[i].reshape(1, d)
        decay = jnp.repeat(ret_decay_logit[i], RET_DK, axis=1)
        w_o = w_out[i].astype(BF16)
        w1 = mlp_w1[i].astype(BF16)
        w2 = mlp_w2[i].astype(BF16)

        pl_lat = _in_proj(x, mod_lat, attn_g, w_nat, w_tr, gw, gb, q_gain, k_gain, lat_tables, rope=True, tm=512)
        pl_ctx = _in_proj(ctx, mod_ctx, attn_g, w_nat, w_tr, gw, gb, q_gain, k_gain, ctx_tables, rope=False,
                          tm=n_ctx)
        cat = lambda j, axis=1: jnp.concatenate([pl_lat[j], pl_ctx[j]], axis=axis)
        rq, rk, rv = cat(0), cat(1), cat(2)
        gq, gk, gv, gf, gbw = cat(4), cat(5), cat(6), cat(8), cat(9)
        ak = cat(11)
        av = cat(12, axis=3)

        ret_f, ret_b = _bidir_scan(rq, rk, rv, None, decay, n_lat=n_lat, tile=scan_tile)
        gla_f, gla_b = _bidir_scan(gq, gk, gv, (gf, gbw), None, n_lat=n_lat, tile=scan_tile)
        y_attn = _attention(pl_lat[10], ak, av, tq=256, tk=3328)

        scans = (ret_f, ret_b, gla_f, gla_b)
        x = _out_proj(x, mod_lat, scans, (pl_lat[3], pl_lat[7]), y_attn, w_o, tm=scan_tile, tok_offset=0)
        x = _mlp(x, mod_lat, mlp_g, w1, w2, final_g, tm=512, final_norm=not need_ctx)
        if need_ctx:
            y_attn_c = _attention(pl_ctx[10], pl_ctx[11], pl_ctx[12], tq=256, tk=256)
            ctx = _out_proj(ctx, mod_ctx, scans, (pl_ctx[3], pl_ctx[7]), y_attn_c, w_o, tm=scan_tile,
                            tok_offset=n_lat // scan_tile)
            ctx = _mlp(ctx, mod_ctx, mlp_g, w1, w2, final_g, tm=n_ctx, final_norm=False)
    return x
```

```python
import functools

import jax
import jax.numpy as jnp
import numpy as np
from jax import lax
from jax.experimental import pallas as pl
from jax.experimental.pallas import tpu as pltpu

F32 = jnp.float32
BF16 = jnp.bfloat16

GRID_W = 64
RET_HEADS = 4
RET_DK = 32
RET_DV = 64
GLA_HEADS = 4
GLA_DK = 32
GLA_DV = 64
GLA_RANK = 16
GLA_TAU = 16.0
GQA_HEADS = 8
GQA_KV_HEADS = 2
GQA_GROUP = 4
GQA_HEAD_DIM = 64
SCAN_CHUNK = 64
ROPE_BASE = 10000.0
NORM_EPS = 1e-6

V7X_VMEM_LIMIT_BYTES = 56 * 1024 * 1024
V_ROWS = 80
ATTN_SUB_KEYS = 128
ATTN_COL_BLOCK = 256
ATTN_LOOKAHEAD = 8
LOG2_E = 1.4426950408889634

NAT_WIDTH = 1664
TR_WIDTH = 768


def _silu(x):
    return x * (1.0 / (1.0 + jnp.exp(-x)))


def _log_sigmoid(z):
    return jnp.minimum(z, 0.0) - jnp.log(1.0 + jnp.exp(-jnp.abs(z)))


def _dot(a, b):
    return jnp.dot(a, b, preferred_element_type=F32)


def _dot_nt(a, b):
    return lax.dot_general(a, b, (((1,), (1,)), ((), ())), preferred_element_type=F32)


def _dot_tn(a, b):
    return lax.dot_general(a, b, (((0,), (0,)), ((), ())), preferred_element_type=F32)


def _mod_kernel(c_ref, w_ref, b_ref, o_ref):
    s = _silu(c_ref[...]).astype(BF16)
    o_ref[0] = _dot(s, w_ref[0].astype(BF16)) + b_ref[0]


def _modulation(cvec, mod_w, mod_b):
    depth, d, n = mod_w.shape
    tn = 1536
    return pl.pallas_call(
        _mod_kernel,
        grid=(depth, n // tn),
        in_specs=[pl.BlockSpec((8, d), lambda l, j: (0, 0)),
                  pl.BlockSpec((1, d, tn), lambda l, j: (l, 0, j)),
                  pl.BlockSpec((1, 1, tn), lambda l, j: (l, 0, j))],
        out_specs=pl.BlockSpec((1, 8, tn), lambda l, j: (l, 0, j)),
        out_shape=jax.ShapeDtypeStruct((depth, 8, n), F32),
        compiler_params=pltpu.CompilerParams(dimension_semantics=("parallel", "parallel"),
                                             vmem_limit_bytes=V7X_VMEM_LIMIT_BYTES),
        name="modulation",
    )(cvec, mod_w, mod_b.reshape(depth, 1, n))


def _rope_nat(x, cos, sin_signed, nf):
    lane = lax.broadcasted_iota(jnp.int32, x.shape, 1)
    first = (lane % (2 * nf)) < nf
    partner = jnp.where(first, pltpu.roll(x, 128 - nf, 1), pltpu.roll(x, nf, 1))
    return x * cos + partner * sin_signed


def _head_norm_rope_t(xt, g_col, cos_t, sin_t, rope):
    h = xt.shape[0] // GQA_HEAD_DIM
    x3 = xt.reshape(h, GQA_HEAD_DIM, xt.shape[1])
    ms = jnp.sum(x3 * x3, axis=1, keepdims=True) * (1.0 / GQA_HEAD_DIM)
    x3 = x3 * lax.rsqrt(ms + NORM_EPS) * g_col[None]
    if rope:
        swapped = jnp.concatenate([x3[:, 16:32], x3[:, 0:16], x3[:, 48:64], x3[:, 32:48]], axis=1)
        x3 = x3 * cos_t[None] + swapped * sin_t[None]
    return x3


def _in_proj_kernel(x_ref, mod_ref, g_ref, wn_ref, wt_ref, gw_ref, gb_ref, qg_ref, kg_ref,
                    cr_ref, sr_ref, ct_ref, st_ref,
                    rq_ref, rk_ref, rv_ref, rg_ref, gq_ref, gk_ref, gv_ref, gg_ref, gf_ref, gbw_ref,
                    aq_ref, ak_ref, av_ref, *, rope):
    x = x_ref[0]
    ms = jnp.mean(x * x, axis=-1, keepdims=True)
    h = x * lax.rsqrt(ms + NORM_EPS) * g_ref[...]
    h = h * (1.0 + mod_ref[0, 1:2, :]) + mod_ref[0, 0:1, :]
    hb = h.astype(BF16)
    p = _dot(hb, wn_ref[...])
    pt = _dot_nt(wt_ref[...], hb)

    rq = p[:, 0:128]
    rk = p[:, 128:256]
    gq = p[:, 768:896]
    if rope:
        cr = cr_ref[...]
        sr = sr_ref[...]
        rq = _rope_nat(rq, cr, sr, RET_DK // 4)
        rk = _rope_nat(rk, cr, sr, RET_DK // 4)
    rq_ref[0] = rq
    rk_ref[0] = rk * (RET_DK ** -0.5)
    rv_ref[0] = p[:, 256:512]
    rg_ref[0] = _silu(p[:, 512:768])
    gq_ref[0] = gq * (GLA_DK ** -0.5)
    gk_ref[0] = p[:, 896:1024]
    gv_ref[0] = p[:, 1024:1280]
    gg_ref[0] = _silu(p[:, 1280:1536])
    z = _dot(p[:, 1536:1664].astype(BF16), gw_ref[...]) + gb_ref[...]
    gates = _log_sigmoid(z) * (1.0 / GLA_TAU)
    gf_ref[0] = gates[:, 0:128]
    gbw_ref[0] = gates[:, 128:256]

    ct = ct_ref[...]
    st = st_ref[...]
    q3 = _head_norm_rope_t(pt[0:512], qg_ref[...], ct, st, rope)
    aq_ref[0] = (q3 * (GQA_HEAD_DIM ** -0.5 * LOG2_E)).astype(BF16)
    k3 = _head_norm_rope_t(pt[512:640], kg_ref[...], ct, st, rope)
    ak_ref[0] = k3.reshape(GQA_KV_HEADS * GQA_HEAD_DIM, k3.shape[2]).T.astype(BF16)
    av_ref[0] = pt[640:768].reshape(GQA_KV_HEADS, GQA_HEAD_DIM, pt.shape[1]).astype(BF16)


def _in_proj(x, mod, norm_g, w_nat, w_tr, gate_w_bd, gate_b_cat, q_gain, k_gain, tables, *, rope, tm):
    bsz, n, d = x.shape
    tm = min(tm, n)
    cos_r, sin_r, cos_t, sin_t = tables
    tok = lambda w: pl.BlockSpec((1, tm, w), lambda b, i: (b, i, 0))
    full = lambda a: pl.BlockSpec(a.shape, lambda b, i: (0,) * a.ndim)
    f32_out = lambda w: jax.ShapeDtypeStruct((bsz, n, w), F32)
    outs = pl.pallas_call(
        functools.partial(_in_proj_kernel, rope=rope),
        grid=(bsz, n // tm),
        in_specs=[tok(d),
                  pl.BlockSpec((1, 6, d), lambda b, i: (b, 0, 0)),
                  full(norm_g), full(w_nat), full(w_tr), full(gate_w_bd), full(gate_b_cat),
                  full(q_gain), full(k_gain),
                  pl.BlockSpec((tm, 128), lambda b, i: (i, 0)),
                  pl.BlockSpec((tm, 128), lambda b, i: (i, 0)),
                  pl.BlockSpec((GQA_HEAD_DIM, tm), lambda b, i: (0, i)),
                  pl.BlockSpec((GQA_HEAD_DIM, tm), lambda b, i: (0, i))],
        out_specs=[tok(128), tok(128), tok(256), tok(256), tok(128), tok(128), tok(256), tok(256),
                   tok(128), tok(128),
                   pl.BlockSpec((1, GQA_HEADS, GQA_HEAD_DIM, tm), lambda b, i: (b, 0, 0, i)),
                   tok(128),
                   pl.BlockSpec((1, GQA_KV_HEADS, GQA_HEAD_DIM, tm), lambda b, i: (b, 0, 0, i))],
        out_shape=[f32_out(128), f32_out(128), f32_out(256), f32_out(256),
                   f32_out(128), f32_out(128), f32_out(256), f32_out(256), f32_out(128), f32_out(128),
                   jax.ShapeDtypeStruct((bsz, GQA_HEADS, GQA_HEAD_DIM, n), BF16),
                   jax.ShapeDtypeStruct((bsz, n, 128), BF16),
                   jax.ShapeDtypeStruct((bsz, GQA_KV_HEADS, GQA_HEAD_DIM, n), BF16)],
        compiler_params=pltpu.CompilerParams(dimension_semantics=("parallel", "parallel"),
                                             vmem_limit_bytes=V7X_VMEM_LIMIT_BYTES),
        name="in_proj_rope" if rope else "in_proj_ctx",
    )(x, mod, norm_g, w_nat, w_tr, gate_w_bd, gate_b_cat, q_gain, k_gain, cos_r, sin_r, cos_t, sin_t)
    return outs


def _cumsum_rows(g, reverse):
    c = g.shape[0]
    row = lax.broadcasted_iota(jnp.int32, g.shape, 0)
    b = g
    s = 1
    while s < c:
        if reverse:
            b = b + jnp.where(row < c - s, pltpu.roll(b, c - s, 0), 0.0)
        else:
            b = b + jnp.where(row >= s, pltpu.roll(b, s, 0), 0.0)
        s *= 2
    return b


def _scan_chunk(q, k, v, b, b_end, st_ref, reverse):
    c = q.shape[0]
    qd = (q * jnp.exp(b)).astype(BF16)
    kd = (k * jnp.exp(-b)).astype(BF16)
    ku = (k * jnp.exp(b_end - b)).astype(BF16)
    vb = v.astype(BF16)

    lane_k = lax.broadcasted_iota(jnp.int32, (c, 128), 1) // RET_DK
    lane_v = lax.broadcasted_iota(jnp.int32, (c, 256), 1) // RET_DV
    zk = jnp.zeros_like(kd)
    zv = jnp.zeros_like(vb)
    kd_stack = jnp.concatenate([jnp.where(lane_k == h, kd, zk) for h in range(RET_HEADS)], axis=0)
    v_stack = jnp.concatenate([jnp.where(lane_v == h, vb, zv) for h in range(RET_HEADS)], axis=0)

    a = _dot_nt(qd, kd_stack)
    i = lax.broadcasted_iota(jnp.int32, a.shape, 0)
    j = lax.broadcasted_iota(jnp.int32, a.shape, 1) % c
    keep = (j > i) if reverse else (j <= i)
    a = jnp.where(keep, a, 0.0).astype(BF16)

    st = st_ref[...]
    o = _dot(a, v_stack) + _dot_nt(qd, st.astype(BF16))

    u = _dot_tn(vb, ku)
    r = lax.broadcasted_iota(jnp.int32, u.shape, 0) // RET_DV
    l = lax.broadcasted_iota(jnp.int32, u.shape, 1) // RET_DK
    st_ref[...] = st * jnp.exp(b_end) + jnp.where(r == l, u, 0.0)
    return o


def _scan_kernel(*refs, gated, tile):
    if gated:
        (qf_ref, kf_ref, vf_ref, gf_ref, qb_ref, kb_ref, vb_ref, gb_ref, of_ref, ob_ref, st_ref) = refs
    else:
        (qf_ref, kf_ref, vf_ref, qb_ref, kb_ref, vb_ref, lg_ref, of_ref, ob_ref, st_ref) = refs
    c = SCAN_CHUNK

    @pl.when(pl.program_id(1) == 0)
    def _():
        st_ref[...] = jnp.zeros_like(st_ref)

    n_chunks = tile // c
    for direction in range(2):
        reverse = direction == 1
        q_ref, k_ref, v_ref, o_ref = ((qb_ref, kb_ref, vb_ref, ob_ref) if reverse
                                      else (qf_ref, kf_ref, vf_ref, of_ref))
        order = range(n_chunks - 1, -1, -1) if reverse else range(n_chunks)
        for ci in order:
            rows = pl.ds(ci * c, c)
            if gated:
                g = (gb_ref if reverse else gf_ref)[0, rows, :]
                b = _cumsum_rows(g, reverse)
                b_end = b[0:1, :] if reverse else b[c - 1:c, :]
            else:
                lg = _log_sigmoid(lg_ref[direction:direction + 1, :])
                row = lax.broadcasted_iota(jnp.int32, (c, 128), 0)
                steps = (c - row) if reverse else (row + 1)
                b = steps.astype(F32) * lg
                b_end = float(c) * lg
            o_ref[0, rows, :] = _scan_chunk(q_ref[0, rows, :], k_ref[0, rows, :], v_ref[0, rows, :],
                                            b, b_end, st_ref.at[direction], reverse)


def _bidir_scan(q, k, v, gates, decay_logit, *, n_lat, tile):
    bsz, n, _ = q.shape
    n_tiles = n // tile
    n_lat_tiles = n_lat // tile
    assert n_tiles == n_lat_tiles + 1, "context must be exactly one scan tile"
    fwd = lambda b, s: (b, jnp.where(s == 0, n_lat_tiles, s - 1), 0)
    bwd = lambda b, s: (b, jnp.where(s == 0, n_lat_tiles, n_lat_tiles - s), 0)
    spec = lambda w, im: pl.BlockSpec((1, tile, w), im)
    gated = gates is not None
    if gated:
        args = (q, k, v, gates[0], q, k, v, gates[1])
        in_specs = [spec(128, fwd), spec(128, fwd), spec(256, fwd), spec(128, fwd),
                    spec(128, bwd), spec(128, bwd), spec(256, bwd), spec(128, bwd)]
    else:
        args = (q, k, v, q, k, v, decay_logit)
        in_specs = [spec(128, fwd), spec(128, fwd), spec(256, fwd),
                    spec(128, bwd), spec(128, bwd), spec(256, bwd),
                    pl.BlockSpec(decay_logit.shape, lambda b, s: (0, 0))]
    return pl.pallas_call(
        functools.partial(_scan_kernel, gated=gated, tile=tile),
        grid=(bsz, n_tiles),
        in_specs=in_specs,
        out_specs=[spec(256, fwd), spec(256, bwd)],
        out_shape=[jax.ShapeDtypeStruct((bsz, n, 256), F32)] * 2,
        scratch_shapes=[pltpu.VMEM((2, 256, 128), F32)],
        compiler_params=pltpu.CompilerParams(dimension_semantics=("parallel", "arbitrary"),
                                             vmem_limit_bytes=V7X_VMEM_LIMIT_BYTES),
        name="scan_gla" if gated else "scan_ret",
    )(*args)


def _attn_kernel(q_ref, k_ref, v_ref, o_ref, acc_ref, qpad_ref, *, n_kv_tiles):
    kv = pl.program_id(1)
    tq = q_ref.shape[3]
    tk = k_ref.shape[2]
    width = GQA_GROUP * tq
    qcat = jnp.concatenate([q_ref[0, g] for g in range(GQA_GROUP)], axis=1)
    zero = jnp.zeros_like(qcat)
    qpad_ref[...] = jnp.where(kv == 0, jnp.concatenate([qcat, zero], axis=0),
                              jnp.concatenate([zero, qcat], axis=0))
    acc_ref[...] = jnp.zeros_like(acc_ref)

    def body(j, m_all):
        n_cb = width // ATTN_COL_BLOCK
        m_parts = [m_all[:, cb * ATTN_COL_BLOCK:(cb + 1) * ATTN_COL_BLOCK] for cb in range(n_cb)]
        n_pairs = tk // (2 * ATTN_SUB_KEYS)
        units = [(2 * pr + h, cb) for pr in range(n_pairs) for cb in range(n_cb) for h in range(2)]
        rows = lambda sub, n=1: slice(sub * ATTN_SUB_KEYS, (sub + n) * ATTN_SUB_KEYS)
        cols = lambda cb: slice(cb * ATTN_COL_BLOCK, (cb + 1) * ATTN_COL_BLOCK)

        def scores(u):
            sub, cb = units[u]
            return _dot(k_ref[0, j, rows(sub), :], qpad_ref[:, cols(cb)])

        pending = [scores(u) for u in range(min(ATTN_LOOKAHEAD, len(units)))]
        for u in range(0, len(units), 2):
            sub, cb = units[u]
            for ahead in (u + ATTN_LOOKAHEAD, u + ATTN_LOOKAHEAD + 1):
                if ahead < len(units):
                    pending.append(scores(ahead))
            s_a = pending.pop(0)
            s_b = pending.pop(0)
            m_a = jnp.maximum(m_parts[cb], jnp.max(s_a, axis=0, keepdims=True))
            p_a = jnp.exp2((s_a - m_a).astype(BF16))
            m_b = jnp.maximum(m_a, jnp.max(s_b, axis=0, keepdims=True))
            p_b = jnp.exp2((s_b - m_b).astype(BF16))
            p = jnp.concatenate([p_a * jnp.exp2(m_a - m_b).astype(BF16), p_b], axis=0)
            alpha = jnp.exp2(m_parts[cb] - m_b)
            acc_ref[:, cols(cb)] = acc_ref[:, cols(cb)] * alpha + _dot(v_ref[0, 0, j, :, rows(sub, 2)], p)
            m_parts[cb] = m_b
        return jnp.concatenate(m_parts, axis=1)

    lax.fori_loop(0, n_kv_tiles, body, jnp.full((1, width), -jnp.inf, F32))

    acc = acc_ref[...]
    o = acc[0:GQA_HEAD_DIM] * (1.0 / acc[GQA_HEAD_DIM:GQA_HEAD_DIM + 1])
    o_ref[0] = jnp.concatenate([o[:, g * tq:(g + 1) * tq].T for g in range(GQA_GROUP)], axis=1).astype(o_ref.dtype)


def _attention(q_t, k_nat, v_t, *, tq, tk):
    bsz, _, _, nq = q_t.shape
    ns = k_nat.shape[1]
    tq = min(tq, nq)
    tk = min(tk, ns)
    n_kv_tiles = ns // tk
    k_tiles = k_nat.reshape(bsz, n_kv_tiles, tk, 128)
    ones = jnp.ones((bsz, GQA_KV_HEADS, V_ROWS - GQA_HEAD_DIM, ns), BF16)
    v_tiles = jnp.concatenate([v_t, ones], axis=2).reshape(bsz, GQA_KV_HEADS, V_ROWS, n_kv_tiles, tk)
    v_tiles = v_tiles.transpose(0, 1, 3, 2, 4)
    return pl.pallas_call(
        functools.partial(_attn_kernel, n_kv_tiles=n_kv_tiles),
        grid=(bsz, GQA_KV_HEADS, nq // tq),
        in_specs=[pl.BlockSpec((1, GQA_GROUP, GQA_HEAD_DIM, tq), lambda b, h, i: (b, h, 0, i)),
                  pl.BlockSpec((1, n_kv_tiles, tk, 128), lambda b, h, i: (b, 0, 0, 0)),
                  pl.BlockSpec((1, 1, n_kv_tiles, V_ROWS, tk), lambda b, h, i: (b, h, 0, 0, 0))],
        out_specs=pl.BlockSpec((1, tq, GQA_GROUP * GQA_HEAD_DIM), lambda b, h, i: (b, i, h)),
        out_shape=jax.ShapeDtypeStruct((bsz, nq, GQA_HEADS * GQA_HEAD_DIM), BF16),
        scratch_shapes=[pltpu.VMEM((V_ROWS, GQA_GROUP * tq), F32),
                        pltpu.VMEM((2 * GQA_HEAD_DIM, GQA_GROUP * tq), BF16)],
        compiler_params=pltpu.CompilerParams(dimension_semantics=("parallel", "parallel", "parallel"),
                                             vmem_limit_bytes=V7X_VMEM_LIMIT_BYTES),
        name="gqa_attention",
    )(q_t, k_tiles, v_tiles)


def _group_mean(x, gmat):
    hi = x.astype(BF16)
    lo = (x - hi.astype(F32)).astype(BF16)
    return (_dot(hi, gmat) + _dot(lo, gmat)) * (1.0 / RET_DV)


def _out_proj_kernel(x_ref, mod_ref, rf_ref, rb_ref, rg_ref, gf_ref, gb_ref, gg_ref, ya_ref, w_ref, o_ref):
    r = lax.broadcasted_iota(jnp.int32, (256, 256), 0) // RET_DV
    c = lax.broadcasted_iota(jnp.int32, (256, 256), 1) // RET_DV
    gmat = jnp.where(r == c, 1.0, 0.0).astype(BF16)

    ro = rf_ref[0] + rb_ref[0]
    rc = ro - _group_mean(ro, gmat)
    y_ret = rg_ref[0] * (rc * lax.rsqrt(_group_mean(rc * rc, gmat) + NORM_EPS))
    go = gf_ref[0] + gb_ref[0]
    y_gla = gg_ref[0] * (go * lax.rsqrt(_group_mean(go * go, gmat) + NORM_EPS))
    y = jnp.concatenate([y_ret.astype(BF16), y_gla.astype(BF16), ya_ref[0]], axis=1)
    o_ref[0] = x_ref[0] + mod_ref[0, 2:3, :] * _dot(y, w_ref[...])


def _out_proj(x, mod, scans, gates, y_attn, w_out, *, tm, tok_offset):
    bsz, n, d = x.shape
    tm = min(tm, n)
    tok = lambda w: pl.BlockSpec((1, tm, w), lambda b, i: (b, i, 0))
    off = lambda w: pl.BlockSpec((1, tm, w), lambda b, i: (b, i + tok_offset, 0))
    rf, rb, gf, gb = scans
    rg, gg = gates
    return pl.pallas_call(
        _out_proj_kernel,
        grid=(bsz, n // tm),
        in_specs=[tok(d), pl.BlockSpec((1, 6, d), lambda b, i: (b, 0, 0)),
                  off(256), off(256), tok(256), off(256), off(256), tok(256), tok(512),
                  pl.BlockSpec(w_out.shape, lambda b, i: (0, 0))],
        out_specs=tok(d),
        out_shape=jax.ShapeDtypeStruct((bsz, n, d), F32),
        compiler_params=pltpu.CompilerParams(dimension_semantics=("parallel", "parallel"),
                                             vmem_limit_bytes=V7X_VMEM_LIMIT_BYTES),
        name="out_proj",
    )(x, mod, rf, rb, rg, gf, gb, gg, y_attn, w_out)


def _mlp_kernel(x_ref, mod_ref, g_ref, w1_ref, w2_ref, fg_ref, o_ref, *, ff_tile, final_norm):
    x = x_ref[0]
    ms = jnp.mean(x * x, axis=-1, keepdims=True)
    h = x * lax.rsqrt(ms + NORM_EPS) * g_ref[...]
    hb = (h * (1.0 + mod_ref[0, 4:5, :]) + mod_ref[0, 3:4, :]).astype(BF16)
    d_ff = w1_ref.shape[1]
    acc = jnp.zeros(x.shape, F32)
    for c in range(d_ff // ff_tile):
        cols = slice(c * ff_tile, (c + 1) * ff_tile)
        a = jnp.maximum(_dot(hb, w1_ref[:, cols]), 0.0)
        acc = acc + _dot((a * a).astype(BF16), w2_ref[cols, :])
    y = x + mod_ref[0, 5:6, :] * acc
    if final_norm:
        y = y * lax.rsqrt(jnp.mean(y * y, axis=-1, keepdims=True) + NORM_EPS) * fg_ref[...]
    o_ref[0] = y


def _mlp(x, mod, norm_g, w1, w2, final_g, *, tm, final_norm):
    bsz, n, d = x.shape
    tm = min(tm, n)
    tok = pl.BlockSpec((1, tm, d), lambda b, i: (b, i, 0))
    full = lambda a: pl.BlockSpec(a.shape, lambda b, i: (0,) * a.ndim, pipeline_mode=pl.Buffered(1))
    return pl.pallas_call(
        functools.partial(_mlp_kernel, ff_tile=1024, final_norm=final_norm),
        grid=(bsz, n // tm),
        in_specs=[tok, pl.BlockSpec((1, 6, d), lambda b, i: (b, 0, 0)),
                  full(norm_g), full(w1), full(w2), full(final_g)],
        out_specs=tok,
        out_shape=jax.ShapeDtypeStruct((bsz, n, d), F32),
        compiler_params=pltpu.CompilerParams(dimension_semantics=("parallel", "parallel"),
                                             vmem_limit_bytes=V7X_VMEM_LIMIT_BYTES),
        name="mlp_final" if final_norm else "mlp",
    )(x, mod, norm_g, w1, w2, final_g)


def _prep_w_in(w):
    d = w.shape[0]
    ret = w[:, 0:768]
    gla = w[:, 768:1536]
    low = w[:, 1536:1568]
    aq = w[:, 1568:2080]
    ak = w[:, 2080:2208]
    av = w[:, 2208:2336]
    w_nat = jnp.concatenate([ret, gla, low, jnp.zeros((d, 128 - 2 * GLA_RANK), w.dtype)], axis=1)
    w_tr = jnp.concatenate([aq, ak, av], axis=1).T
    return w_nat.astype(BF16), w_tr.astype(BF16)


def _rope_tables(n_lat):
    t = jnp.arange(n_lat)
    row = (t // GRID_W).astype(F32)
    col = (t % GRID_W).astype(F32)

    def head_tables(head_dim):
        nf = head_dim // 4
        inv_freq = ROPE_BASE ** (-jnp.arange(nf, dtype=F32) / nf)
        ang_r = row[:, None] * inv_freq
        ang_c = col[:, None] * inv_freq
        cos = jnp.concatenate([jnp.cos(ang_r)] * 2 + [jnp.cos(ang_c)] * 2, axis=1)
        sin = jnp.concatenate([-jnp.sin(ang_r), jnp.sin(ang_r), -jnp.sin(ang_c), jnp.sin(ang_c)], axis=1)
        return cos, sin

    cos_r, sin_r = head_tables(RET_DK)
    cos_r = jnp.tile(cos_r, (1, RET_HEADS))
    sin_r = jnp.tile(sin_r, (1, RET_HEADS))
    cos_a, sin_a = head_tables(GQA_HEAD_DIM)
    return cos_r, sin_r, cos_a.T, sin_a.T


def kernel(x, c, ctx, c_ctx, mod_w, mod_b, attn_norm_g, mlp_norm_g, w_in, w_out, ret_decay_logit,
           gla_gate_w, gla_gate_b, qk_norm_g, mlp_w1, mlp_w2, final_norm_g):
    bsz, n_lat, d = x.shape
    n_ctx = ctx.shape[1]
    depth = mod_w.shape[0]
    scan_tile = n_ctx
    assert n_lat % scan_tile == 0 and scan_tile % SCAN_CHUNK == 0 and bsz <= 7

    cvec = jnp.zeros((8, d), F32).at[:bsz].set(c).at[bsz].set(c_ctx)
    mods = _modulation(cvec, mod_w, mod_b)
    lat_tables = _rope_tables(n_lat)
    ctx_tables = tuple(jnp.zeros((n_ctx, 128), F32) for _ in range(2)) + \
        tuple(jnp.zeros((GQA_HEAD_DIM, n_ctx), F32) for _ in range(2))
    final_g = final_norm_g.reshape(1, d)

    for i in range(depth):
        need_ctx = i < depth - 1
        mod_lat = mods[i, :bsz].reshape(bsz, 6, d)
        mod_ctx = jnp.broadcast_to(mods[i, bsz].reshape(1, 6, d), (bsz, 6, d))
        w_nat, w_tr = _prep_w_in(w_in[i])
        gw = jnp.zeros((128, 256), F32)
        gw = gw.at[0:GLA_RANK, 0:128].set(gla_gate_w[i, 0]).at[GLA_RANK:2 * GLA_RANK, 128:256].set(gla_gate_w[i, 1])
        gw = gw.astype(BF16)
        gb = gla_gate_b[i].reshape(1, 256)
        q_gain = qk_norm_g[i, 0].reshape(GQA_HEAD_DIM, 1)
        k_gain = qk_norm_g[i, 1].reshape(GQA_HEAD_DIM, 1)
        attn_g = attn_norm_g[i].reshape(1, d)
        mlp_g = mlp_norm_g[i].reshape(1, d)
        decay = jnp.repeat(ret_decay_logit[i], RET_DK, axis=1)
        w_o = w_out[i].astype(BF16)
        w1 = mlp_w1[i].astype(BF16)
        w2 = mlp_w2[i].astype(BF16)

        pl_lat = _in_proj(x, mod_lat, attn_g, w_nat, w_tr, gw, gb, q_gain, k_gain, lat_tables, rope=True, tm=512)
        pl_ctx = _in_proj(ctx, mod_ctx, attn_g, w_nat, w_tr, gw, gb, q_gain, k_gain, ctx_tables, rope=False,
                          tm=n_ctx)
        cat = lambda j, axis=1: jnp.concatenate([pl_lat[j], pl_ctx[j]], axis=axis)
        rq, rk, rv = cat(0), cat(1), cat(2)
        gq, gk, gv, gf, gbw = cat(4), cat(5), cat(6), cat(8), cat(9)
        ak = cat(11)
        av = cat(12, axis=3)

        ret_f, ret_b = _bidir_scan(rq, rk, rv, None, decay, n_lat=n_lat, tile=scan_tile)
        gla_f, gla_b = _bidir_scan(gq, gk, gv, (gf, gbw), None, n_lat=n_lat, tile=scan_tile)
        y_attn = _attention(pl_lat[10], ak, av, tq=256, tk=3328)

        scans = (ret_f, ret_b, gla_f, gla_b)
        x = _out_proj(x, mod_lat, scans, (pl_lat[3], pl_lat[7]), y_attn, w_o, tm=scan_tile, tok_offset=0)
        x = _mlp(x, mod_lat, mlp_g, w1, w2, final_g, tm=512, final_norm=not need_ctx)
        if need_ctx:
            y_attn_c = _attention(pl_ctx[10], pl_ctx[11], pl_ctx[12], tq=256, tk=256)
            ctx = _out_proj(ctx, mod_ctx, scans, (pl_ctx[3], pl_ctx[7]), y_attn_c, w_o, tm=scan_tile,
                            tok_offset=n_lat // scan_tile)
            ctx = _mlp(ctx, mod_ctx, mlp_g, w1, w2, final_g, tm=n_ctx, final_norm=False)
    return x
```

```python
import functools

import jax
import jax.numpy as jnp
import numpy as np
from jax import lax
from jax.experimental import pallas as pl
from jax.experimental.pallas import tpu as pltpu

F32 = jnp.float32
BF16 = jnp.bfloat16

GRID_W = 64
RET_HEADS = 4
RET_DK = 32
RET_DV = 64
GLA_HEADS = 4
GLA_DK = 32
GLA_DV = 64
GLA_RANK = 16
GLA_TAU = 16.0
GQA_HEADS = 8
GQA_KV_HEADS = 2
GQA_GROUP = 4
GQA_HEAD_DIM = 64
SCAN_CHUNK = 64
ROPE_BASE = 10000.0
NORM_EPS = 1e-6

V7X_VMEM_LIMIT_BYTES = 56 * 1024 * 1024
V_ROWS = 80
ATTN_SUB_KEYS = 128
ATTN_COL_BLOCK = 256
ATTN_LOOKAHEAD = 8
LOG2_E = 1.4426950408889634

NAT_WIDTH = 1664
TR_WIDTH = 768


def _silu(x):
    return x * (1.0 / (1.0 + jnp.exp(-x)))


def _log_sigmoid(z):
    return jnp.minimum(z, 0.0) - jnp.log(1.0 + jnp.exp(-jnp.abs(z)))


def _dot(a, b):
    return jnp.dot(a, b, preferred_element_type=F32)


def _dot_nt(a, b):
    return lax.dot_general(a, b, (((1,), (1,)), ((), ())), preferred_element_type=F32)


def _dot_tn(a, b):
    return lax.dot_general(a, b, (((0,), (0,)), ((), ())), preferred_element_type=F32)


def _mod_kernel(c_ref, w_ref, b_ref, o_ref):
    s = _silu(c_ref[...]).astype(BF16)
    o_ref[0] = _dot(s, w_ref[0].astype(BF16)) + b_ref[0]


def _modulation(cvec, mod_w, mod_b):
    depth, d, n = mod_w.shape
    tn = 1536
    return pl.pallas_call(
        _mod_kernel,
        grid=(depth, n // tn),
        in_specs=[pl.BlockSpec((8, d), lambda l, j: (0, 0)),
                  pl.BlockSpec((1, d, tn), lambda l, j: (l, 0, j)),
                  pl.BlockSpec((1, 1, tn), lambda l, j: (l, 0, j))],
        out_specs=pl.BlockSpec((1, 8, tn), lambda l, j: (l, 0, j)),
        out_shape=jax.ShapeDtypeStruct((depth, 8, n), F32),
        compiler_params=pltpu.CompilerParams(dimension_semantics=("parallel", "parallel"),
                                             vmem_limit_bytes=V7X_VMEM_LIMIT_BYTES),
        name="modulation",
    )(cvec, mod_w, mod_b.reshape(depth, 1, n))


def _rope_nat(x, cos, sin_signed, nf):
    lane = lax.broadcasted_iota(jnp.int32, x.shape, 1)
    first = (lane % (2 * nf)) < nf
    partner = jnp.where(first, pltpu.roll(x, 128 - nf, 1), pltpu.roll(x, nf, 1))
    return x * cos + partner * sin_signed


def _head_norm_rope_t(xt, g_col, cos_t, sin_t, rope):
    h = xt.shape[0] // GQA_HEAD_DIM
    x3 = xt.reshape(h, GQA_HEAD_DIM, xt.shape[1])
    ms = jnp.sum(x3 * x3, axis=1, keepdims=True) * (1.0 / GQA_HEAD_DIM)
    x3 = x3 * lax.rsqrt(ms + NORM_EPS) * g_col[None]
    if rope:
        swapped = jnp.concatenate([x3[:, 16:32], x3[:, 0:16], x3[:, 48:64], x3[:, 32:48]], axis=1)
        x3 = x3 * cos_t[None] + swapped * sin_t[None]
    return x3


def _in_proj_kernel(x_ref, mod_ref, g_ref, wn_ref, wt_ref, gw_ref, gb_ref, qg_ref, kg_ref,
                    cr_ref, sr_ref, ct_ref, st_ref,
                    rq_ref, rk_ref, rv_ref, rg_ref, gq_ref, gk_ref, gv_ref, gg_ref, gf_ref, gbw_ref,
                    aq_ref, ak_ref, av_ref, *, rope):
    x = x_ref[0]
    ms = jnp.mean(x * x, axis=-1, keepdims=True)
    h = x * lax.rsqrt(ms + NORM_EPS) * g_ref[...]
    h = h * (1.0 + mod_ref[0, 1:2, :]) + mod_ref[0, 0:1, :]
    hb = h.astype(BF16)
    p = _dot(hb, wn_ref[...])
    pt = _dot_nt(wt_ref[...], hb)

    rq = p[:, 0:128]
    rk = p[:, 128:256]
    gq = p[:, 768:896]
    if rope:
        cr = cr_ref[...]
        sr = sr_ref[...]
        rq = _rope_nat(rq, cr, sr, RET_DK // 4)
        rk = _rope_nat(rk, cr, sr, RET_DK // 4)
    rq_ref[0] = rq
    rk_ref[0] = rk * (RET_DK ** -0.5)
    rv_ref[0] = p[:, 256:512]
    rg_ref[0] = _silu(p[:, 512:768])
    gq_ref[0] = gq * (GLA_DK ** -0.5)
    gk_ref[0] = p[:, 896:1024]
    gv_ref[0] = p[:, 1024:1280]
    gg_ref[0] = _silu(p[:, 1280:1536])
    z = _dot(p[:, 1536:1664].astype(BF16), gw_ref[...]) + gb_ref[...]
    gates = _log_sigmoid(z) * (1.0 / GLA_TAU)
    gf_ref[0] = gates[:, 0:128]
    gbw_ref[0] = gates[:, 128:256]

    ct = ct_ref[...]
    st = st_ref[...]
    q3 = _head_norm_rope_t(pt[0:512], qg_ref[...], ct, st, rope)
    aq_ref[0] = (q3 * (GQA_HEAD_DIM ** -0.5 * LOG2_E)).astype(BF16)
    k3 = _head_norm_rope_t(pt[512:640], kg_ref[...], ct, st, rope)
    ak_ref[0] = k3.reshape(GQA_KV_HEADS * GQA_HEAD_DIM, k3.shape[2]).T.astype(BF16)
    av_ref[0] = pt[640:768].reshape(GQA_KV_HEADS, GQA_HEAD_DIM, pt.shape[1]).astype(BF16)


def _in_proj(x, mod, norm_g, w_nat, w_tr, gate_w_bd, gate_b_cat, q_gain, k_gain, tables, *, rope, tm):
    bsz, n, d = x.shape
    tm = min(tm, n)
    cos_r, sin_r, cos_t, sin_t = tables
    tok = lambda w: pl.BlockSpec((1, tm, w), lambda b, i: (b, i, 0))
    full = lambda a: pl.BlockSpec(a.shape, lambda b, i: (0,) * a.ndim)
    f32_out = lambda w: jax.ShapeDtypeStruct((bsz, n, w), F32)
    outs = pl.pallas_call(
        functools.partial(_in_proj_kernel, rope=rope),
        grid=(bsz, n // tm),
        in_specs=[tok(d),
                  pl.BlockSpec((1, 6, d), lambda b, i: (b, 0, 0)),
                  full(norm_g), full(w_nat), full(w_tr), full(gate_w_bd), full(gate_b_cat),
                  full(q_gain), full(k_gain),
                  pl.BlockSpec((tm, 128), lambda b, i: (i, 0)),
                  pl.BlockSpec((tm, 128), lambda b, i: (i, 0)),
                  pl.BlockSpec((GQA_HEAD_DIM, tm), lambda b, i: (0, i)),
                  pl.BlockSpec((GQA_HEAD_DIM, tm), lambda b, i: (0, i))],
        out_specs=[tok(128), tok(128), tok(256), tok(256), tok(128), tok(128), tok(256), tok(256),
                   tok(128), tok(128),
                   pl.BlockSpec((1, GQA_HEADS, GQA_HEAD_DIM, tm), lambda b, i: (b, 0, 0, i)),
                   tok(128),
                   pl.BlockSpec((1, GQA_KV_HEADS, GQA_HEAD_DIM, tm), lambda b, i: (b, 0, 0, i))],
        out_shape=[f32_out(128), f32_out(128), f32_out(256), f32_out(256),
                   f32_out(128), f32_out(128), f32_out(256), f32_out(256), f32_out(128), f32_out(128),
                   jax.ShapeDtypeStruct((bsz, GQA_HEADS, GQA_HEAD_DIM, n), BF16),
                   jax.ShapeDtypeStruct((bsz, n, 128), BF16),
                   jax.ShapeDtypeStruct((bsz, GQA_KV_HEADS, GQA_HEAD_DIM, n), BF16)],
        compiler_params=pltpu.CompilerParams(dimension_semantics=("parallel", "parallel"),
                                             vmem_limit_bytes=V7X_VMEM_LIMIT_BYTES),
        name="in_proj_rope" if rope else "in_proj_ctx",
    )(x, mod, norm_g, w_nat, w_tr, gate_w_bd, gate_b_cat, q_gain, k_gain, cos_r, sin_r, cos_t, sin_t)
    return outs


def _cumsum_rows(g, reverse):
    c = g.shape[0]
    row = lax.broadcasted_iota(jnp.int32, g.shape, 0)
    b = g
    s = 1
    while s < c:
        if reverse:
            b = b + jnp.where(row < c - s, pltpu.roll(b, c - s, 0), 0.0)
        else:
            b = b + jnp.where(row >= s, pltpu.roll(b, s, 0), 0.0)
        s *= 2
    return b


def _scan_chunk_local(q, k, v, b, b_end, reverse):
    c = q.shape[0]
    qd = (q * jnp.exp(b)).astype(BF16)
    kd = (k * jnp.exp(-b)).astype(BF16)
    ku = (k * jnp.exp(b_end - b)).astype(BF16)
    vb = v.astype(BF16)

    lane_k = lax.broadcasted_iota(jnp.int32, (c, 128), 1) // RET_DK
    lane_v = lax.broadcasted_iota(jnp.int32, (c, 256), 1) // RET_DV
    zk = jnp.zeros_like(kd)
    zv = jnp.zeros_like(vb)
    kd_stack = jnp.concatenate([jnp.where(lane_k == h, kd, zk) for h in range(RET_HEADS)], axis=0)
    v_stack = jnp.concatenate([jnp.where(lane_v == h, vb, zv) for h in range(RET_HEADS)], axis=0)

    a = _dot_nt(qd, kd_stack)
    u = _dot_tn(vb, ku)
    r = lax.broadcasted_iota(jnp.int32, u.shape, 0) // RET_DV
    l = lax.broadcasted_iota(jnp.int32, u.shape, 1) // RET_DK
    return qd, a, v_stack, jnp.where(r == l, u, 0.0), jnp.exp(b_end)


def _scan_chunk_intra(a, v_stack, reverse):
    c = a.shape[0]
    i = lax.broadcasted_iota(jnp.int32, a.shape, 0)
    j = lax.broadcasted_iota(jnp.int32, a.shape, 1) % c
    keep = (j > i) if reverse else (j <= i)
    return _dot(jnp.where(keep, a, 0.0).astype(BF16), v_stack)


def _scan_tile(fwd_refs, bwd_refs, out_refs, st_ref, lg_ref, *, gated, tile):
    c = SCAN_CHUNK
    n_chunks = tile // c
    order = {0: list(range(n_chunks)), 1: list(range(n_chunks - 1, -1, -1))}
    local = {}
    for direction in range(2):
        reverse = direction == 1
        q_ref, k_ref, v_ref, g_ref = bwd_refs if reverse else fwd_refs
        for ci in order[direction]:
            rows = pl.ds(ci * c, c)
            if gated:
                b = _cumsum_rows(g_ref[0, rows, :], reverse)
                b_end = b[0:1, :] if reverse else b[c - 1:c, :]
            else:
                lg = _log_sigmoid(lg_ref[direction:direction + 1, :])
                row = lax.broadcasted_iota(jnp.int32, (c, 128), 0)
                steps = (c - row) if reverse else (row + 1)
                b = steps.astype(F32) * lg
                b_end = float(c) * lg
            local[direction, ci] = _scan_chunk_local(q_ref[0, rows, :], k_ref[0, rows, :], v_ref[0, rows, :],
                                                     b, b_end, reverse)

    for direction in range(2):
        for ci in order[direction]:
            qd, a, v_stack, u, decay = local[direction, ci]
            out_refs[direction][0, pl.ds(ci * c, c), :] = _scan_chunk_intra(a, v_stack, direction == 1)
            local[direction, ci] = (qd, u, decay)

    start_state = {}
    for direction in range(2):
        st = st_ref[direction]
        for ci in order[direction]:
            _, u, decay = local[direction, ci]
            start_state[direction, ci] = st.astype(BF16)
            st = st * decay + u
        st_ref[direction] = st

    for direction in range(2):
        o_ref = out_refs[direction]
        for ci in order[direction]:
            rows = pl.ds(ci * c, c)
            o_ref[0, rows, :] = o_ref[0, rows, :] + _dot_nt(local[direction, ci][0], start_state[direction, ci])


def _scan_kernel(*refs, gated, tile):
    if gated:
        (qf, kf, vf, gf, qb, kb, vb, gb, qc, kc, vc, gfc, gbc, of, ob, ocf, ocb, st_ref) = refs
        lg_ref = None
    else:
        (qf, kf, vf, qb, kb, vb, qc, kc, vc, lg_ref, of, ob, ocf, ocb, st_ref) = refs
        gf = gb = gfc = gbc = None
    step = pl.program_id(1)
    tile_fn = functools.partial(_scan_tile, st_ref=st_ref, lg_ref=lg_ref, gated=gated, tile=tile)

    @pl.when(step == 0)
    def _():
        st_ref[...] = jnp.zeros_like(st_ref)
        tile_fn((qc, kc, vc, gfc), (qc, kc, vc, gbc), (ocf, ocb))

    @pl.when(step > 0)
    def _():
        tile_fn((qf, kf, vf, gf), (qb, kb, vb, gb), (of, ob))


def _bidir_scan(lat, ctx, gates_lat, gates_ctx, decay_logit, *, tile):
    q, k, v = lat
    bsz, n_lat, _ = q.shape
    n_lat_tiles = n_lat // tile
    assert ctx[0].shape[1] == tile, "context must be exactly one scan tile"
    fwd = lambda b, s: (b, jnp.maximum(s - 1, 0), 0)
    bwd = lambda b, s: (b, n_lat_tiles - 1 - jnp.maximum(s - 1, 0), 0)
    one = lambda b, s: (b, 0, 0)
    spec = lambda w, im: pl.BlockSpec((1, tile, w), im)
    qkv = lambda im: [spec(128, im), spec(128, im), spec(256, im)]
    gated = gates_lat is not None
    if gated:
        args = (q, k, v, gates_lat[0], q, k, v, gates_lat[1], *ctx, *gates_ctx)
        in_specs = (qkv(fwd) + [spec(128, fwd)] + qkv(bwd) + [spec(128, bwd)] + qkv(one)
                    + [spec(128, one), spec(128, one)])
    else:
        args = (q, k, v, q, k, v, *ctx, decay_logit)
        in_specs = qkv(fwd) + qkv(bwd) + qkv(one) + [pl.BlockSpec(decay_logit.shape, lambda b, s: (0, 0))]
    return pl.pallas_call(
        functools.partial(_scan_kernel, gated=gated, tile=tile),
        grid=(bsz, n_lat_tiles + 1),
        in_specs=in_specs,
        out_specs=[spec(256, fwd), spec(256, bwd), spec(256, one), spec(256, one)],
        out_shape=([jax.ShapeDtypeStruct((bsz, n_lat, 256), F32)] * 2
                   + [jax.ShapeDtypeStruct((bsz, tile, 256), F32)] * 2),
        scratch_shapes=[pltpu.VMEM((2, 256, 128), F32)],
        compiler_params=pltpu.CompilerParams(dimension_semantics=("parallel", "arbitrary"),
                                             vmem_limit_bytes=V7X_VMEM_LIMIT_BYTES),
        name="scan_gla" if gated else "scan_ret",
    )(*args)


def _attn_kernel(q_ref, k_ref, v_ref, o_ref, acc_ref, qpad_ref, *, n_kv_tiles):
    kv = pl.program_id(1)
    tq = q_ref.shape[3]
    tk = k_ref.shape[2]
    width = GQA_GROUP * tq
    qcat = jnp.concatenate([q_ref[0, g] for g in range(GQA_GROUP)], axis=1)
    zero = jnp.zeros_like(qcat)
    qpad_ref[...] = jnp.where(kv == 0, jnp.concatenate([qcat, zero], axis=0),
                              jnp.concatenate([zero, qcat], axis=0))
    acc_ref[...] = jnp.zeros_like(acc_ref)

    def body(j, m_all):
        n_cb = width // ATTN_COL_BLOCK
        m_parts = [m_all[:, cb * ATTN_COL_BLOCK:(cb + 1) * ATTN_COL_BLOCK] for cb in range(n_cb)]
        n_pairs = tk // (2 * ATTN_SUB_KEYS)
        units = [(2 * pr + h, cb) for pr in range(n_pairs) for cb in range(n_cb) for h in range(2)]
        rows = lambda sub, n=1: slice(sub * ATTN_SUB_KEYS, (sub + n) * ATTN_SUB_KEYS)
        cols = lambda cb: slice(cb * ATTN_COL_BLOCK, (cb + 1) * ATTN_COL_BLOCK)

        def scores(u):
            sub, cb = units[u]
            return _dot(k_ref[0, j, rows(sub), :], qpad_ref[:, cols(cb)])

        pending = [scores(u) for u in range(min(ATTN_LOOKAHEAD, len(units)))]
        for u in range(0, len(units), 2):
            sub, cb = units[u]
            for ahead in (u + ATTN_LOOKAHEAD, u + ATTN_LOOKAHEAD + 1):
                if ahead < len(units):
                    pending.append(scores(ahead))
            s_a = pending.pop(0)
            s_b = pending.pop(0)
            m_a = jnp.maximum(m_parts[cb], jnp.max(s_a, axis=0, keepdims=True))
            p_a = jnp.exp2((s_a - m_a).astype(BF16))
            m_b = jnp.maximum(m_a, jnp.max(s_b, axis=0, keepdims=True))
            p_b = jnp.exp2((s_b - m_b).astype(BF16))
            p = jnp.concatenate([p_a * jnp.exp2(m_a - m_b).astype(BF16), p_b], axis=0)
            alpha = jnp.exp2(m_parts[cb] - m_b)
            acc_ref[:, cols(cb)] = acc_ref[:, cols(cb)] * alpha + _dot(v_ref[0, 0, j, :, rows(sub, 2)], p)
            m_parts[cb] = m_b
        return jnp.concatenate(m_parts, axis=1)

    lax.fori_loop(0, n_kv_tiles, body, jnp.full((1, width), -jnp.inf, F32))

    acc = acc_ref[...]
    o = acc[0:GQA_HEAD_DIM] * (1.0 / acc[GQA_HEAD_DIM:GQA_HEAD_DIM + 1])
    o_ref[0] = jnp.concatenate([o[:, g * tq:(g + 1) * tq].T for g in range(GQA_GROUP)], axis=1).astype(o_ref.dtype)


def _attention(q_t, k_nat, v_t, *, tq, tk):
    bsz, _, _, nq = q_t.shape
    ns = k_nat.shape[1]
    tq = min(tq, nq)
    tk = min(tk, ns)
    n_kv_tiles = ns // tk
    k_tiles = k_nat.reshape(bsz, n_kv_tiles, tk, 128)
    ones = jnp.ones((bsz, GQA_KV_HEADS, V_ROWS - GQA_HEAD_DIM, ns), BF16)
    v_tiles = jnp.concatenate([v_t, ones], axis=2).reshape(bsz, GQA_KV_HEADS, V_ROWS, n_kv_tiles, tk)
    v_tiles = v_tiles.transpose(0, 1, 3, 2, 4)
    return pl.pallas_call(
        functools.partial(_attn_kernel, n_kv_tiles=n_kv_tiles),
        grid=(bsz, GQA_KV_HEADS, nq // tq),
        in_specs=[pl.BlockSpec((1, GQA_GROUP, GQA_HEAD_DIM, tq), lambda b, h, i: (b, h, 0, i)),
                  pl.BlockSpec((1, n_kv_tiles, tk, 128), lambda b, h, i: (b, 0, 0, 0)),
                  pl.BlockSpec((1, 1, n_kv_tiles, V_ROWS, tk), lambda b, h, i: (b, h, 0, 0, 0))],
        out_specs=pl.BlockSpec((1, tq, GQA_GROUP * GQA_HEAD_DIM), lambda b, h, i: (b, i, h)),
        out_shape=jax.ShapeDtypeStruct((bsz, nq, GQA_HEADS * GQA_HEAD_DIM), BF16),
        scratch_shapes=[pltpu.VMEM((V_ROWS, GQA_GROUP * tq), F32),
                        pltpu.VMEM((2 * GQA_HEAD_DIM, GQA_GROUP * tq), BF16)],
        compiler_params=pltpu.CompilerParams(dimension_semantics=("parallel", "parallel", "parallel"),
                                             vmem_limit_bytes=V7X_VMEM_LIMIT_BYTES),
        name="gqa_attention",
    )(q_t, k_tiles, v_tiles)


def _group_mean(x, gmat):
    hi = x.astype(BF16)
    lo = (x - hi.astype(F32)).astype(BF16)
    return (_dot(hi, gmat) + _dot(lo, gmat)) * (1.0 / RET_DV)


def _post_kernel(x_ref, mod_ref, rf_ref, rb_ref, rg_ref, gf_ref, gb_ref, gg_ref, ya_ref, wo_ref,
                 g_ref, w1_ref, w2_ref, fg_ref, o_ref, *, ff_tile, final_norm):
    r = lax.broadcasted_iota(jnp.int32, (256, 256), 0) // RET_DV
    c = lax.broadcasted_iota(jnp.int32, (256, 256), 1) // RET_DV
    gmat = jnp.where(r == c, 1.0, 0.0).astype(BF16)
    ro = rf_ref[0] + rb_ref[0]
    rc = ro - _group_mean(ro, gmat)
    y_ret = rg_ref[0] * (rc * lax.rsqrt(_group_mean(rc * rc, gmat) + NORM_EPS))
    go = gf_ref[0] + gb_ref[0]
    y_gla = gg_ref[0] * (go * lax.rsqrt(_group_mean(go * go, gmat) + NORM_EPS))
    y = jnp.concatenate([y_ret.astype(BF16), y_gla.astype(BF16), ya_ref[0]], axis=1)
    x = x_ref[0] + mod_ref[0, 2:3, :] * _dot(y, wo_ref[...])

    ms = jnp.mean(x * x, axis=-1, keepdims=True)
    h = x * lax.rsqrt(ms + NORM_EPS) * g_ref[...]
    hb = (h * (1.0 + mod_ref[0, 4:5, :]) + mod_ref[0, 3:4, :]).astype(BF16)
    d_ff = w1_ref.shape[1]
    acc = jnp.zeros(x.shape, F32)
    for c in range(d_ff // ff_tile):
        cols = slice(c * ff_tile, (c + 1) * ff_tile)
        a = jnp.maximum(_dot(hb, w1_ref[:, cols]), 0.0)
        acc = acc + _dot((a * a).astype(BF16), w2_ref[cols, :])
    y = x + mod_ref[0, 5:6, :] * acc
    if final_norm:
        y = y * lax.rsqrt(jnp.mean(y * y, axis=-1, keepdims=True) + NORM_EPS) * fg_ref[...]
    o_ref[0] = y


def _post(x, mod, scans, gates, y_attn, w_out, norm_g, w1, w2, final_g, *, tm, final_norm):
    bsz, n, d = x.shape
    tm = min(tm, n)
    tok = lambda w: pl.BlockSpec((1, tm, w), lambda b, i: (b, i, 0))
    full = lambda a: pl.BlockSpec(a.shape, lambda b, i: (0,) * a.ndim, pipeline_mode=pl.Buffered(1))
    rf, rb, gf, gb = scans
    rg, gg = gates
    return pl.pallas_call(
        functools.partial(_post_kernel, ff_tile=1024, final_norm=final_norm),
        grid=(bsz, n // tm),
        in_specs=[tok(d), pl.BlockSpec((1, 6, d), lambda b, i: (b, 0, 0)),
                  tok(256), tok(256), tok(256), tok(256), tok(256), tok(256), tok(512),
                  full(w_out), full(norm_g), full(w1), full(w2), full(final_g)],
        out_specs=tok(d),
        out_shape=jax.ShapeDtypeStruct((bsz, n, d), F32),
        compiler_params=pltpu.CompilerParams(dimension_semantics=("parallel", "parallel"),
                                             vmem_limit_bytes=V7X_VMEM_LIMIT_BYTES),
        name="post_final" if final_norm else "post",
    )(x, mod, rf, rb, rg, gf, gb, gg, y_attn, w_out, norm_g, w1, w2, final_g)


def _prep_w_in(w):
    d = w.shape[0]
    ret = w[:, 0:768]
    gla = w[:, 768:1536]
    low = w[:, 1536:1568]
    aq = w[:, 1568:2080]
    ak = w[:, 2080:2208]
    av = w[:, 2208:2336]
    w_nat = jnp.concatenate([ret, gla, low, jnp.zeros((d, 128 - 2 * GLA_RANK), w.dtype)], axis=1)
    w_tr = jnp.concatenate([aq, ak, av], axis=1).T
    return w_nat.astype(BF16), w_tr.astype(BF16)


def _rope_tables(n_lat):
    t = jnp.arange(n_lat)
    row = (t // GRID_W).astype(F32)
    col = (t % GRID_W).astype(F32)

    def head_tables(head_dim):
        nf = head_dim // 4
        inv_freq = ROPE_BASE ** (-jnp.arange(nf, dtype=F32) / nf)
        ang_r = row[:, None] * inv_freq
        ang_c = col[:, None] * inv_freq
        cos = jnp.concatenate([jnp.cos(ang_r)] * 2 + [jnp.cos(ang_c)] * 2, axis=1)
        sin = jnp.concatenate([-jnp.sin(ang_r), jnp.sin(ang_r), -jnp.sin(ang_c), jnp.sin(ang_c)], axis=1)
        return cos, sin

    cos_r, sin_r = head_tables(RET_DK)
    cos_r = jnp.tile(cos_r, (1, RET_HEADS))
    sin_r = jnp.tile(sin_r, (1, RET_HEADS))
    cos_a, sin_a = head_tables(GQA_HEAD_DIM)
    return cos_r, sin_r, cos_a.T, sin_a.T


def kernel(x, c, ctx, c_ctx, mod_w, mod_b, attn_norm_g, mlp_norm_g, w_in, w_out, ret_decay_logit,
           gla_gate_w, gla_gate_b, qk_norm_g, mlp_w1, mlp_w2, final_norm_g):
    bsz, n_lat, d = x.shape
    n_ctx = ctx.shape[1]
    depth = mod_w.shape[0]
    scan_tile = n_ctx
    assert n_lat % scan_tile == 0 and scan_tile % SCAN_CHUNK == 0 and bsz <= 7

    cvec = jnp.zeros((8, d), F32).at[:bsz].set(c).at[bsz].set(c_ctx)
    mods = _modulation(cvec, mod_w, mod_b)
    lat_tables = _rope_tables(n_lat)
    ctx_tables = tuple(jnp.zeros((n_ctx, 128), F32) for _ in range(2)) + \
        tuple(jnp.zeros((GQA_HEAD_DIM, n_ctx), F32) for _ in range(2))
    final_g = final_norm_g.reshape(1, d)

    for i in range(depth):
        need_ctx = i < depth - 1
        mod_lat = mods[i, :bsz].reshape(bsz, 6, d)
        mod_ctx = jnp.broadcast_to(mods[i, bsz].reshape(1, 6, d), (bsz, 6, d))
        w_nat, w_tr = _prep_w_in(w_in[i])
        gw = jnp.zeros((128, 256), F32)
        gw = gw.at[0:GLA_RANK, 0:128].set(gla_gate_w[i, 0]).at[GLA_RANK:2 * GLA_RANK, 128:256].set(gla_gate_w[i, 1])
        gw = gw.astype(BF16)
        gb = gla_gate_b[i].reshape(1, 256)
        q_gain = qk_norm_g[i, 0].reshape(GQA_HEAD_DIM, 1)
        k_gain = qk_norm_g[i, 1].reshape(GQA_HEAD_DIM, 1)
        attn_g = attn_norm_g[i].reshape(1, d)
        mlp_g = mlp_norm_g[i].reshape(1, d)
        decay = jnp.repeat(ret_decay_logit[i], RET_DK, axis=1)
        w_o = w_out[i].astype(BF16)
        w1 = mlp_w1[i].astype(BF16)
        w2 = mlp_w2[i].astype(BF16)

        pl_lat = _in_proj(x, mod_lat, attn_g, w_nat, w_tr, gw, gb, q_gain, k_gain, lat_tables, rope=True, tm=512)
        pl_ctx = _in_proj(ctx, mod_ctx, attn_g, w_nat, w_tr, gw, gb, q_gain, k_gain, ctx_tables, rope=False,
                          tm=n_ctx)
        ret = _bidir_scan(pl_lat[0:3], pl_ctx[0:3], None, None, decay, tile=scan_tile)
        gla = _bidir_scan(pl_lat[4:7], pl_ctx[4:7], pl_lat[8:10], pl_ctx[8:10], None, tile=scan_tile)
        ak = jnp.concatenate([pl_lat[11], pl_ctx[11]], axis=1)
        av = jnp.concatenate([pl_lat[12], pl_ctx[12]], axis=3)
        y_attn = _attention(pl_lat[10], ak, av, tq=256, tk=3328)

        x = _post(x, mod_lat, (ret[0], ret[1], gla[0], gla[1]), (pl_lat[3], pl_lat[7]), y_attn, w_o,
                  mlp_g, w1, w2, final_g, tm=512, final_norm=not need_ctx)
        if need_ctx:
            y_attn_c = _attention(pl_ctx[10], pl_ctx[11], pl_ctx[12], tq=256, tk=256)
            ctx = _post(ctx, mod_ctx, (ret[2], ret[3], gla[2], gla[3]), (pl_ctx[3], pl_ctx[7]), y_attn_c, w_o,
                        mlp_g, w1, w2, final_g, tm=n_ctx, final_norm=False)
    return x
```

```python
import functools

import jax
import jax.numpy as jnp
import numpy as np
from jax import lax
from jax.experimental import pallas as pl
from jax.experimental.pallas import tpu as pltpu

F32 = jnp.float32
BF16 = jnp.bfloat16

GRID_W = 64
RET_HEADS = 4
RET_DK = 32
RET_DV = 64
GLA_HEADS = 4
GLA_DK = 32
GLA_DV = 64
GLA_RANK = 16
GLA_TAU = 16.0
GQA_HEADS = 8
GQA_KV_HEADS = 2
GQA_GROUP = 4
GQA_HEAD_DIM = 64
SCAN_CHUNK = 64
ROPE_BASE = 10000.0
NORM_EPS = 1e-6

V7X_VMEM_LIMIT_BYTES = 56 * 1024 * 1024
V_ROWS = 80
ATTN_SUB_KEYS = 128
ATTN_COL_BLOCK = 256
ATTN_LOOKAHEAD = 8
LOG2_E = 1.4426950408889634

NAT_WIDTH = 1664
TR_WIDTH = 768


def _silu(x):
    return x * (1.0 / (1.0 + jnp.exp(-x)))


def _log_sigmoid(z):
    return jnp.minimum(z, 0.0) - jnp.log(1.0 + jnp.exp(-jnp.abs(z)))


def _dot(a, b):
    return jnp.dot(a, b, preferred_element_type=F32)


def _dot_nt(a, b):
    return lax.dot_general(a, b, (((1,), (1,)), ((), ())), preferred_element_type=F32)


def _dot_tn(a, b):
    return lax.dot_general(a, b, (((0,), (0,)), ((), ())), preferred_element_type=F32)


def _mod_kernel(c_ref, w_ref, b_ref, o_ref):
    s = _silu(c_ref[...]).astype(BF16)
    o_ref[0] = _dot(s, w_ref[0].astype(BF16)) + b_ref[0]


def _modulation(cvec, mod_w, mod_b):
    depth, d, n = mod_w.shape
    tn = 1536
    return pl.pallas_call(
        _mod_kernel,
        grid=(depth, n // tn),
        in_specs=[pl.BlockSpec((8, d), lambda l, j: (0, 0)),
                  pl.BlockSpec((1, d, tn), lambda l, j: (l, 0, j)),
                  pl.BlockSpec((1, 1, tn), lambda l, j: (l, 0, j))],
        out_specs=pl.BlockSpec((1, 8, tn), lambda l, j: (l, 0, j)),
        out_shape=jax.ShapeDtypeStruct((depth, 8, n), F32),
        compiler_params=pltpu.CompilerParams(dimension_semantics=("parallel", "parallel"),
                                             vmem_limit_bytes=V7X_VMEM_LIMIT_BYTES),
        name="modulation",
    )(cvec, mod_w, mod_b.reshape(depth, 1, n))


def _rope_nat(x, cos, sin_signed, nf):
    lane = lax.broadcasted_iota(jnp.int32, x.shape, 1)
    first = (lane % (2 * nf)) < nf
    partner = jnp.where(first, pltpu.roll(x, 128 - nf, 1), pltpu.roll(x, nf, 1))
    return x * cos + partner * sin_signed


def _head_norm_rope_t(xt, g_col, cos_t, sin_t, rope):
    h = xt.shape[0] // GQA_HEAD_DIM
    x3 = xt.reshape(h, GQA_HEAD_DIM, xt.shape[1])
    ms = jnp.sum(x3 * x3, axis=1, keepdims=True) * (1.0 / GQA_HEAD_DIM)
    x3 = x3 * lax.rsqrt(ms + NORM_EPS) * g_col[None]
    if rope:
        swapped = jnp.concatenate([x3[:, 16:32], x3[:, 0:16], x3[:, 48:64], x3[:, 32:48]], axis=1)
        x3 = x3 * cos_t[None] + swapped * sin_t[None]
    return x3


def _in_proj_kernel(x_ref, mod_ref, g_ref, wn_ref, wt_ref, gw_ref, gb_ref, qg_ref, kg_ref,
                    cr_ref, sr_ref, ct_ref, st_ref,
                    rq_ref, rk_ref, rv_ref, rg_ref, gq_ref, gk_ref, gv_ref, gg_ref, gf_ref, gbw_ref,
                    aq_ref, ak_ref, av_ref, *, rope):
    x = x_ref[0]
    ms = jnp.mean(x * x, axis=-1, keepdims=True)
    h = x * lax.rsqrt(ms + NORM_EPS) * g_ref[...]
    h = h * (1.0 + mod_ref[0, 1:2, :]) + mod_ref[0, 0:1, :]
    hb = h.astype(BF16)
    p = _dot(hb, wn_ref[...])
    pt = _dot_nt(wt_ref[...], hb)

    rq = p[:, 0:128]
    rk = p[:, 128:256]
    gq = p[:, 768:896]
    if rope:
        cr = cr_ref[...]
        sr = sr_ref[...]
        rq = _rope_nat(rq, cr, sr, RET_DK // 4)
        rk = _rope_nat(rk, cr, sr, RET_DK // 4)
    rq_ref[0] = rq
    rk_ref[0] = rk * (RET_DK ** -0.5)
    rv_ref[0] = p[:, 256:512]
    rg_ref[0] = _silu(p[:, 512:768])
    gq_ref[0] = gq * (GLA_DK ** -0.5)
    gk_ref[0] = p[:, 896:1024]
    gv_ref[0] = p[:, 1024:1280]
    gg_ref[0] = _silu(p[:, 1280:1536])
    z = _dot(p[:, 1536:1664].astype(BF16), gw_ref[...]) + gb_ref[...]
    gates = _log_sigmoid(z) * (1.0 / GLA_TAU)
    gf_ref[0] = gates[:, 0:128]
    gbw_ref[0] = gates[:, 128:256]

    ct = ct_ref[...]
    st = st_ref[...]
    q3 = _head_norm_rope_t(pt[0:512], qg_ref[...], ct, st, rope)
    aq_ref[0] = (q3 * (GQA_HEAD_DIM ** -0.5 * LOG2_E)).astype(BF16)
    k3 = _head_norm_rope_t(pt[512:640], kg_ref[...], ct, st, rope)
    ak_ref[0] = k3.reshape(GQA_KV_HEADS * GQA_HEAD_DIM, k3.shape[2]).T.astype(BF16)
    av_ref[0] = pt[640:768].reshape(GQA_KV_HEADS, GQA_HEAD_DIM, pt.shape[1]).astype(BF16)


def _in_proj(x, mod, norm_g, w_nat, w_tr, gate_w_bd, gate_b_cat, q_gain, k_gain, tables, *, rope, tm):
    bsz, n, d = x.shape
    tm = min(tm, n)
    cos_r, sin_r, cos_t, sin_t = tables
    tok = lambda w: pl.BlockSpec((1, tm, w), lambda b, i: (b, i, 0))
    full = lambda a: pl.BlockSpec(a.shape, lambda b, i: (0,) * a.ndim)
    f32_out = lambda w: jax.ShapeDtypeStruct((bsz, n, w), F32)
    outs = pl.pallas_call(
        functools.partial(_in_proj_kernel, rope=rope),
        grid=(bsz, n // tm),
        in_specs=[tok(d),
                  pl.BlockSpec((1, 6, d), lambda b, i: (b, 0, 0)),
                  full(norm_g), full(w_nat), full(w_tr), full(gate_w_bd), full(gate_b_cat),
                  full(q_gain), full(k_gain),
                  pl.BlockSpec((tm, 128), lambda b, i: (i, 0)),
                  pl.BlockSpec((tm, 128), lambda b, i: (i, 0)),
                  pl.BlockSpec((GQA_HEAD_DIM, tm), lambda b, i: (0, i)),
                  pl.BlockSpec((GQA_HEAD_DIM, tm), lambda b, i: (0, i))],
        out_specs=[tok(128), tok(128), tok(256), tok(256), tok(128), tok(128), tok(256), tok(256),
                   tok(128), tok(128),
                   pl.BlockSpec((1, GQA_HEADS, GQA_HEAD_DIM, tm), lambda b, i: (b, 0, 0, i)),
                   tok(128),
                   pl.BlockSpec((1, GQA_KV_HEADS, GQA_HEAD_DIM, tm), lambda b, i: (b, 0, 0, i))],
        out_shape=[f32_out(128), f32_out(128), f32_out(256), f32_out(256),
                   f32_out(128), f32_out(128), f32_out(256), f32_out(256), f32_out(128), f32_out(128),
                   jax.ShapeDtypeStruct((bsz, GQA_HEADS, GQA_HEAD_DIM, n), BF16),
                   jax.ShapeDtypeStruct((bsz, n, 128), BF16),
                   jax.ShapeDtypeStruct((bsz, GQA_KV_HEADS, GQA_HEAD_DIM, n), BF16)],
        compiler_params=pltpu.CompilerParams(dimension_semantics=("parallel", "parallel"),
                                             vmem_limit_bytes=V7X_VMEM_LIMIT_BYTES),
        name="in_proj_rope" if rope else "in_proj_ctx",
    )(x, mod, norm_g, w_nat, w_tr, gate_w_bd, gate_b_cat, q_gain, k_gain, cos_r, sin_r, cos_t, sin_t)
    return outs


def _cumsum_rows(g, reverse):
    c = g.shape[0]
    row = lax.broadcasted_iota(jnp.int32, g.shape, 0)
    b = g
    s = 1
    while s < c:
        if reverse:
            b = b + jnp.where(row < c - s, pltpu.roll(b, c - s, 0), 0.0)
        else:
            b = b + jnp.where(row >= s, pltpu.roll(b, s, 0), 0.0)
        s *= 2
    return b


def _scan_chunk_local(q, k, v, b, b_end, reverse):
    c = q.shape[0]
    qd = (q * jnp.exp(b)).astype(BF16)
    kd = (k * jnp.exp(-b)).astype(BF16)
    ku = (k * jnp.exp(b_end - b)).astype(BF16)
    vb = v.astype(BF16)

    lane_k = lax.broadcasted_iota(jnp.int32, (c, 128), 1) // RET_DK
    lane_v = lax.broadcasted_iota(jnp.int32, (c, 256), 1) // RET_DV
    zk = jnp.zeros_like(kd)
    zv = jnp.zeros_like(vb)
    kd_stack = jnp.concatenate([jnp.where(lane_k == h, kd, zk) for h in range(RET_HEADS)], axis=0)
    v_stack = jnp.concatenate([jnp.where(lane_v == h, vb, zv) for h in range(RET_HEADS)], axis=0)

    a = _dot_nt(qd, kd_stack)
    u = _dot_tn(vb, ku)
    r = lax.broadcasted_iota(jnp.int32, u.shape, 0) // RET_DV
    l = lax.broadcasted_iota(jnp.int32, u.shape, 1) // RET_DK
    return qd, a, v_stack, jnp.where(r == l, u, 0.0), jnp.exp(b_end)


def _scan_chunk_intra(a, v_stack, reverse):
    c = a.shape[0]
    i = lax.broadcasted_iota(jnp.int32, a.shape, 0)
    j = lax.broadcasted_iota(jnp.int32, a.shape, 1) % c
    keep = (j > i) if reverse else (j <= i)
    return _dot(jnp.where(keep, a, 0.0).astype(BF16), v_stack)


def _scan_tile(fwd_refs, bwd_refs, out_refs, st_ref, lg_ref, *, gated, tile):
    c = SCAN_CHUNK
    n_chunks = tile // c
    order = {0: list(range(n_chunks)), 1: list(range(n_chunks - 1, -1, -1))}
    local = {}
    for direction in range(2):
        reverse = direction == 1
        q_ref, k_ref, v_ref, g_ref = bwd_refs if reverse else fwd_refs
        for ci in order[direction]:
            rows = pl.ds(ci * c, c)
            if gated:
                b = _cumsum_rows(g_ref[0, rows, :], reverse)
                b_end = b[0:1, :] if reverse else b[c - 1:c, :]
            else:
                lg = _log_sigmoid(lg_ref[direction:direction + 1, :])
                row = lax.broadcasted_iota(jnp.int32, (c, 128), 0)
                steps = (c - row) if reverse else (row + 1)
                b = steps.astype(F32) * lg
                b_end = float(c) * lg
            local[direction, ci] = _scan_chunk_local(q_ref[0, rows, :], k_ref[0, rows, :], v_ref[0, rows, :],
                                                     b, b_end, reverse)

    for direction in range(2):
        for ci in order[direction]:
            qd, a, v_stack, u, decay = local[direction, ci]
            out_refs[direction][0, pl.ds(ci * c, c), :] = _scan_chunk_intra(a, v_stack, direction == 1)
            local[direction, ci] = (qd, u, decay)

    start_state = {}
    for direction in range(2):
        st = st_ref[direction]
        for ci in order[direction]:
            _, u, decay = local[direction, ci]
            start_state[direction, ci] = st.astype(BF16)
            st = st * decay + u
        st_ref[direction] = st

    for direction in range(2):
        o_ref = out_refs[direction]
        for ci in order[direction]:
            rows = pl.ds(ci * c, c)
            o_ref[0, rows, :] = o_ref[0, rows, :] + _dot_nt(local[direction, ci][0], start_state[direction, ci])


def _scan_kernel(*refs, gated, tile):
    if gated:
        (qf, kf, vf, gf, qb, kb, vb, gb, qc, kc, vc, gfc, gbc, of, ob, ocf, ocb, st_ref) = refs
        lg_ref = None
    else:
        (qf, kf, vf, qb, kb, vb, qc, kc, vc, lg_ref, of, ob, ocf, ocb, st_ref) = refs
        gf = gb = gfc = gbc = None
    step = pl.program_id(1)
    tile_fn = functools.partial(_scan_tile, st_ref=st_ref, lg_ref=lg_ref, gated=gated, tile=tile)

    @pl.when(step == 0)
    def _():
        st_ref[...] = jnp.zeros_like(st_ref)
        tile_fn((qc, kc, vc, gfc), (qc, kc, vc, gbc), (ocf, ocb))

    @pl.when(step > 0)
    def _():
        tile_fn((qf, kf, vf, gf), (qb, kb, vb, gb), (of, ob))


def _bidir_scan(lat, ctx, gates_lat, gates_ctx, decay_logit, *, tile):
    q, k, v = lat
    bsz, n_lat, _ = q.shape
    n_lat_tiles = n_lat // tile
    assert ctx[0].shape[1] == tile, "context must be exactly one scan tile"
    fwd = lambda b, s: (b, jnp.maximum(s - 1, 0), 0)
    bwd = lambda b, s: (b, n_lat_tiles - 1 - jnp.maximum(s - 1, 0), 0)
    one = lambda b, s: (b, 0, 0)
    spec = lambda w, im: pl.BlockSpec((1, tile, w), im)
    qkv = lambda im: [spec(128, im), spec(128, im), spec(256, im)]
    gated = gates_lat is not None
    if gated:
        args = (q, k, v, gates_lat[0], q, k, v, gates_lat[1], *ctx, *gates_ctx)
        in_specs = (qkv(fwd) + [spec(128, fwd)] + qkv(bwd) + [spec(128, bwd)] + qkv(one)
                    + [spec(128, one), spec(128, one)])
    else:
        args = (q, k, v, q, k, v, *ctx, decay_logit)
        in_specs = qkv(fwd) + qkv(bwd) + qkv(one) + [pl.BlockSpec(decay_logit.shape, lambda b, s: (0, 0))]
    return pl.pallas_call(
        functools.partial(_scan_kernel, gated=gated, tile=tile),
        grid=(bsz, n_lat_tiles + 1),
        in_specs=in_specs,
        out_specs=[spec(256, fwd), spec(256, bwd), spec(256, one), spec(256, one)],
        out_shape=([jax.ShapeDtypeStruct((bsz, n_lat, 256), F32)] * 2
                   + [jax.ShapeDtypeStruct((bsz, tile, 256), F32)] * 2),
        scratch_shapes=[pltpu.VMEM((2, 256, 128), F32)],
        compiler_params=pltpu.CompilerParams(dimension_semantics=("parallel", "arbitrary"),
                                             vmem_limit_bytes=V7X_VMEM_LIMIT_BYTES),
        name="scan_gla" if gated else "scan_ret",
    )(*args)


def _attn_kernel(q_ref, k_ref, v_ref, o_ref, acc_ref, qpad_ref, *, n_kv_tiles):
    kv = pl.program_id(1)
    tq = q_ref.shape[3]
    tk = k_ref.shape[2]
    width = GQA_GROUP * tq
    qcat = jnp.concatenate([q_ref[0, g] for g in range(GQA_GROUP)], axis=1)
    zero = jnp.zeros_like(qcat)
    qpad_ref[...] = jnp.where(kv == 0, jnp.concatenate([qcat, zero], axis=0),
                              jnp.concatenate([zero, qcat], axis=0))
    acc_ref[...] = jnp.zeros_like(acc_ref)

    def body(j, m_all):
        n_cb = width // ATTN_COL_BLOCK
        m_parts = [m_all[:, cb * ATTN_COL_BLOCK:(cb + 1) * ATTN_COL_BLOCK] for cb in range(n_cb)]
        n_pairs = tk // (2 * ATTN_SUB_KEYS)
        units = [(2 * pr + h, cb) for pr in range(n_pairs) for cb in range(n_cb) for h in range(2)]
        rows = lambda sub, n=1: slice(sub * ATTN_SUB_KEYS, (sub + n) * ATTN_SUB_KEYS)
        cols = lambda cb: slice(cb * ATTN_COL_BLOCK, (cb + 1) * ATTN_COL_BLOCK)

        def scores(u):
            sub, cb = units[u]
            return _dot(k_ref[0, j, rows(sub), :], qpad_ref[:, cols(cb)])

        pending = [scores(u) for u in range(min(ATTN_LOOKAHEAD, len(units)))]
        for u in range(0, len(units), 2):
            sub, cb = units[u]
            for ahead in (u + ATTN_LOOKAHEAD, u + ATTN_LOOKAHEAD + 1):
                if ahead < len(units):
                    pending.append(scores(ahead))
            s_a = pending.pop(0).astype(BF16)
            s_b = pending.pop(0).astype(BF16)
            m_a = jnp.maximum(m_parts[cb], jnp.max(s_a, axis=0, keepdims=True))
            p_a = jnp.exp2(s_a - m_a)
            m_b = jnp.maximum(m_a, jnp.max(s_b, axis=0, keepdims=True))
            p_b = jnp.exp2(s_b - m_b)
            p = jnp.concatenate([p_a * jnp.exp2(m_a - m_b), p_b], axis=0)
            alpha = jnp.exp2(m_parts[cb].astype(F32) - m_b.astype(F32))
            acc_ref[:, cols(cb)] = acc_ref[:, cols(cb)] * alpha + _dot(v_ref[0, 0, j, :, rows(sub, 2)], p)
            m_parts[cb] = m_b
        return jnp.concatenate(m_parts, axis=1)

    lax.fori_loop(0, n_kv_tiles, body, jnp.full((1, width), -jnp.inf, BF16))

    acc = acc_ref[...]
    o = acc[0:GQA_HEAD_DIM] * (1.0 / acc[GQA_HEAD_DIM:GQA_HEAD_DIM + 1])
    o_ref[0] = jnp.concatenate([o[:, g * tq:(g + 1) * tq].T for g in range(GQA_GROUP)], axis=1).astype(o_ref.dtype)


def _attention(q_t, k_nat, v_t, *, tq, tk):
    bsz, _, _, nq = q_t.shape
    ns = k_nat.shape[1]
    tq = min(tq, nq)
    tk = min(tk, ns)
    n_kv_tiles = ns // tk
    k_tiles = k_nat.reshape(bsz, n_kv_tiles, tk, 128)
    ones = jnp.ones((bsz, GQA_KV_HEADS, V_ROWS - GQA_HEAD_DIM, ns), BF16)
    v_tiles = jnp.concatenate([v_t, ones], axis=2).reshape(bsz, GQA_KV_HEADS, V_ROWS, n_kv_tiles, tk)
    v_tiles = v_tiles.transpose(0, 1, 3, 2, 4)
    return pl.pallas_call(
        functools.partial(_attn_kernel, n_kv_tiles=n_kv_tiles),
        grid=(bsz, GQA_KV_HEADS, nq // tq),
        in_specs=[pl.BlockSpec((1, GQA_GROUP, GQA_HEAD_DIM, tq), lambda b, h, i: (b, h, 0, i)),
                  pl.BlockSpec((1, n_kv_tiles, tk, 128), lambda b, h, i: (b, 0, 0, 0)),
                  pl.BlockSpec((1, 1, n_kv_tiles, V_ROWS, tk), lambda b, h, i: (b, h, 0, 0, 0))],
        out_specs=pl.BlockSpec((1, tq, GQA_GROUP * GQA_HEAD_DIM), lambda b, h, i: (b, i, h)),
        out_shape=jax.ShapeDtypeStruct((bsz, nq, GQA_HEADS * GQA_HEAD_DIM), BF16),
        scratch_shapes=[pltpu.VMEM((V_ROWS, GQA_GROUP * tq), F32),
                        pltpu.VMEM((2 * GQA_HEAD_DIM, GQA_GROUP * tq), BF16)],
        compiler_params=pltpu.CompilerParams(dimension_semantics=("parallel", "parallel", "parallel"),
                                             vmem_limit_bytes=V7X_VMEM_LIMIT_BYTES),
        name="gqa_attention",
    )(q_t, k_tiles, v_tiles)


def _group_mean(x, gmat):
    hi = x.astype(BF16)
    lo = (x - hi.astype(F32)).astype(BF16)
    return (_dot(hi, gmat) + _dot(lo, gmat)) * (1.0 / RET_DV)


def _post_kernel(x_ref, mod_ref, rf_ref, rb_ref, rg_ref, gf_ref, gb_ref, gg_ref, ya_ref, wo_ref,
                 g_ref, w1_ref, w2_ref, fg_ref, o_ref, *, ff_tile, final_norm):
    r = lax.broadcasted_iota(jnp.int32, (256, 256), 0) // RET_DV
    c = lax.broadcasted_iota(jnp.int32, (256, 256), 1) // RET_DV
    gmat = jnp.where(r == c, 1.0, 0.0).astype(BF16)
    ro = rf_ref[0] + rb_ref[0]
    rc = ro - _group_mean(ro, gmat)
    y_ret = rg_ref[0] * (rc * lax.rsqrt(_group_mean(rc * rc, gmat) + NORM_EPS))
    go = gf_ref[0] + gb_ref[0]
    y_gla = gg_ref[0] * (go * lax.rsqrt(_group_mean(go * go, gmat) + NORM_EPS))
    y = jnp.concatenate([y_ret.astype(BF16), y_gla.astype(BF16), ya_ref[0]], axis=1)
    x = x_ref[0] + mod_ref[0, 2:3, :] * _dot(y, wo_ref[...])

    ms = jnp.mean(x * x, axis=-1, keepdims=True)
    h = x * lax.rsqrt(ms + NORM_EPS) * g_ref[...]
    hb = (h * (1.0 + mod_ref[0, 4:5, :]) + mod_ref[0, 3:4, :]).astype(BF16)
    d_ff = w1_ref.shape[1]
    acc = jnp.zeros(x.shape, F32)
    for c in range(d_ff // ff_tile):
        cols = slice(c * ff_tile, (c + 1) * ff_tile)
        a = jnp.maximum(_dot(hb, w1_ref[:, cols]), 0.0)
        acc = acc + _dot((a * a).astype(BF16), w2_ref[cols, :])
    y = x + mod_ref[0, 5:6, :] * acc
    if final_norm:
        y = y * lax.rsqrt(jnp.mean(y * y, axis=-1, keepdims=True) + NORM_EPS) * fg_ref[...]
    o_ref[0] = y


def _post(x, mod, scans, gates, y_attn, w_out, norm_g, w1, w2, final_g, *, tm, final_norm):
    bsz, n, d = x.shape
    tm = min(tm, n)
    tok = lambda w: pl.BlockSpec((1, tm, w), lambda b, i: (b, i, 0))
    full = lambda a: pl.BlockSpec(a.shape, lambda b, i: (0,) * a.ndim, pipeline_mode=pl.Buffered(1))
    rf, rb, gf, gb = scans
    rg, gg = gates
    return pl.pallas_call(
        functools.partial(_post_kernel, ff_tile=1024, final_norm=final_norm),
        grid=(bsz, n // tm),
        in_specs=[tok(d), pl.BlockSpec((1, 6, d), lambda b, i: (b, 0, 0)),
                  tok(256), tok(256), tok(256), tok(256), tok(256), tok(256), tok(512),
                  full(w_out), full(norm_g), full(w1), full(w2), full(final_g)],
        out_specs=tok(d),
        out_shape=jax.ShapeDtypeStruct((bsz, n, d), F32),
        compiler_params=pltpu.CompilerParams(dimension_semantics=("parallel", "parallel"),
                                             vmem_limit_bytes=V7X_VMEM_LIMIT_BYTES),
        name="post_final" if final_norm else "post",
    )(x, mod, rf, rb, rg, gf, gb, gg, y_attn, w_out, norm_g, w1, w2, final_g)


def _prep_w_in(w):
    d = w.shape[0]
    ret = w[:, 0:768]
    gla = w[:, 768:1536]
    low = w[:, 1536:1568]
    aq = w[:, 1568:2080]
    ak = w[:, 2080:2208]
    av = w[:, 2208:2336]
    w_nat = jnp.concatenate([ret, gla, low, jnp.zeros((d, 128 - 2 * GLA_RANK), w.dtype)], axis=1)
    w_tr = jnp.concatenate([aq, ak, av], axis=1).T
    return w_nat.astype(BF16), w_tr.astype(BF16)


def _rope_tables(n_lat):
    t = jnp.arange(n_lat)
    row = (t // GRID_W).astype(F32)
    col = (t % GRID_W).astype(F32)

    def head_tables(head_dim):
        nf = head_dim // 4
        inv_freq = ROPE_BASE ** (-jnp.arange(nf, dtype=F32) / nf)
        ang_r = row[:, None] * inv_freq
        ang_c = col[:, None] * inv_freq
        cos = jnp.concatenate([jnp.cos(ang_r)] * 2 + [jnp.cos(ang_c)] * 2, axis=1)
        sin = jnp.concatenate([-jnp.sin(ang_r), jnp.sin(ang_r), -jnp.sin(ang_c), jnp.sin(ang_c)], axis=1)
        return cos, sin

    cos_r, sin_r = head_tables(RET_DK)
    cos_r = jnp.tile(cos_r, (1, RET_HEADS))
    sin_r = jnp.tile(sin_r, (1, RET_HEADS))
    cos_a, sin_a = head_tables(GQA_HEAD_DIM)
    return cos_r, sin_r, cos_a.T, sin_a.T


def kernel(x, c, ctx, c_ctx, mod_w, mod_b, attn_norm_g, mlp_norm_g, w_in, w_out, ret_decay_logit,
           gla_gate_w, gla_gate_b, qk_norm_g, mlp_w1, mlp_w2, final_norm_g):
    bsz, n_lat, d = x.shape
    n_ctx = ctx.shape[1]
    depth = mod_w.shape[0]
    scan_tile = n_ctx
    assert n_lat % scan_tile == 0 and scan_tile % SCAN_CHUNK == 0 and bsz <= 7

    cvec = jnp.zeros((8, d), F32).at[:bsz].set(c).at[bsz].set(c_ctx)
    mods = _modulation(cvec, mod_w, mod_b)
    lat_tables = _rope_tables(n_lat)
    ctx_tables = tuple(jnp.zeros((n_ctx, 128), F32) for _ in range(2)) + \
        tuple(jnp.zeros((GQA_HEAD_DIM, n_ctx), F32) for _ in range(2))
    final_g = final_norm_g.reshape(1, d)

    for i in range(depth):
        need_ctx = i < depth - 1
        mod_lat = mods[i, :bsz].reshape(bsz, 6, d)
        mod_ctx = jnp.broadcast_to(mods[i, bsz].reshape(1, 6, d), (bsz, 6, d))
        w_nat, w_tr = _prep_w_in(w_in[i])
        gw = jnp.zeros((128, 256), F32)
        gw = gw.at[0:GLA_RANK, 0:128].set(gla_gate_w[i, 0]).at[GLA_RANK:2 * GLA_RANK, 128:256].set(gla_gate_w[i, 1])
        gw = gw.astype(BF16)
        gb = gla_gate_b[i].reshape(1, 256)
        q_gain = qk_norm_g[i, 0].reshape(GQA_HEAD_DIM, 1)
        k_gain = qk_norm_g[i, 1].reshape(GQA_HEAD_DIM, 1)
        attn_g = attn_norm_g[i].reshape(1, d)
        mlp_g = mlp_norm_g[i].reshape(1, d)
        decay = jnp.repeat(ret_decay_logit[i], RET_DK, axis=1)
        w_o = w_out[i].astype(BF16)
        w1 = mlp_w1[i].astype(BF16)
        w2 = mlp_w2[i].astype(BF16)

        pl_lat = _in_proj(x, mod_lat, attn_g, w_nat, w_tr, gw, gb, q_gain, k_gain, lat_tables, rope=True, tm=512)
        pl_ctx = _in_proj(ctx, mod_ctx, attn_g, w_nat, w_tr, gw, gb, q_gain, k_gain, ctx_tables, rope=False,
                          tm=n_ctx)
        ret = _bidir_scan(pl_lat[0:3], pl_ctx[0:3], None, None, decay, tile=scan_tile)
        gla = _bidir_scan(pl_lat[4:7], pl_ctx[4:7], pl_lat[8:10], pl_ctx[8:10], None, tile=scan_tile)
        ak = jnp.concatenate([pl_lat[11], pl_ctx[11]], axis=1)
        av = jnp.concatenate([pl_lat[12], pl_ctx[12]], axis=3)
        y_attn = _attention(pl_lat[10], ak, av, tq=256, tk=3328)

        x = _post(x, mod_lat, (ret[0], ret[1], gla[0], gla[1]), (pl_lat[3], pl_lat[7]), y_attn, w_o,
                  mlp_g, w1, w2, final_g, tm=512, final_norm=not need_ctx)
        if need_ctx:
            y_attn_c = _attention(pl_ctx[10], pl_ctx[11], pl_ctx[12], tq=256, tk=256)
            ctx = _post(ctx, mod_ctx, (ret[2], ret[3], gla[2], gla[3]), (pl_ctx[3], pl_ctx[7]), y_attn_c, w_o,
                        mlp_g, w1, w2, final_g, tm=n_ctx, final_norm=False)
    return x
```

```python
import functools

import jax
import jax.numpy as jnp
import numpy as np
from jax import lax
from jax.experimental import pallas as pl
from jax.experimental.pallas import tpu as pltpu

F32 = jnp.float32
BF16 = jnp.bfloat16

GRID_W = 64
RET_HEADS = 4
RET_DK = 32
RET_DV = 64
GLA_HEADS = 4
GLA_DK = 32
GLA_DV = 64
GLA_RANK = 16
GLA_TAU = 16.0
GQA_HEADS = 8
GQA_KV_HEADS = 2
GQA_GROUP = 4
GQA_HEAD_DIM = 64
SCAN_CHUNK = 64
ROPE_BASE = 10000.0
NORM_EPS = 1e-6

V7X_VMEM_LIMIT_BYTES = 56 * 1024 * 1024
V_ROWS = 80
ATTN_SUB_KEYS = 128
ATTN_COL_BLOCK = 256
ATTN_LOOKAHEAD = 8
LOG2_E = 1.4426950408889634

NAT_WIDTH = 1664
TR_WIDTH = 768


def _silu(x):
    return x * (1.0 / (1.0 + jnp.exp(-x)))


def _log_sigmoid(z):
    return jnp.minimum(z, 0.0) - jnp.log(1.0 + jnp.exp(-jnp.abs(z)))


def _dot(a, b):
    return jnp.dot(a, b, preferred_element_type=F32)


def _dot_nt(a, b):
    return lax.dot_general(a, b, (((1,), (1,)), ((), ())), preferred_element_type=F32)


def _dot_tn(a, b):
    return lax.dot_general(a, b, (((0,), (0,)), ((), ())), preferred_element_type=F32)


def _mod_kernel(c_ref, w_ref, b_ref, o_ref):
    s = _silu(c_ref[...]).astype(BF16)
    o_ref[0] = _dot(s, w_ref[0].astype(BF16)) + b_ref[0]


def _modulation(cvec, mod_w, mod_b):
    depth, d, n = mod_w.shape
    tn = 1536
    return pl.pallas_call(
        _mod_kernel,
        grid=(depth, n // tn),
        in_specs=[pl.BlockSpec((8, d), lambda l, j: (0, 0)),
                  pl.BlockSpec((1, d, tn), lambda l, j: (l, 0, j)),
                  pl.BlockSpec((1, 1, tn), lambda l, j: (l, 0, j))],
        out_specs=pl.BlockSpec((1, 8, tn), lambda l, j: (l, 0, j)),
        out_shape=jax.ShapeDtypeStruct((depth, 8, n), F32),
        compiler_params=pltpu.CompilerParams(dimension_semantics=("parallel", "parallel"),
                                             vmem_limit_bytes=V7X_VMEM_LIMIT_BYTES),
        name="modulation",
    )(cvec, mod_w, mod_b.reshape(depth, 1, n))


def _rope_nat(x, cos, sin_signed, nf):
    lane = lax.broadcasted_iota(jnp.int32, x.shape, 1)
    first = (lane % (2 * nf)) < nf
    partner = jnp.where(first, pltpu.roll(x, 128 - nf, 1), pltpu.roll(x, nf, 1))
    return x * cos + partner * sin_signed


def _head_norm_rope_t(xt, g_col, cos_t, sin_t, rope):
    h = xt.shape[0] // GQA_HEAD_DIM
    x3 = xt.reshape(h, GQA_HEAD_DIM, xt.shape[1])
    ms = jnp.sum(x3 * x3, axis=1, keepdims=True) * (1.0 / GQA_HEAD_DIM)
    x3 = x3 * lax.rsqrt(ms + NORM_EPS) * g_col[None]
    if rope:
        swapped = jnp.concatenate([x3[:, 16:32], x3[:, 0:16], x3[:, 48:64], x3[:, 32:48]], axis=1)
        x3 = x3 * cos_t[None] + swapped * sin_t[None]
    return x3


def _in_proj_kernel(x_ref, mod_ref, g_ref, wn_ref, wt_ref, gw_ref, gb_ref, qg_ref, kg_ref,
                    cr_ref, sr_ref, ct_ref, st_ref,
                    rq_ref, rk_ref, rv_ref, rg_ref, gq_ref, gk_ref, gv_ref, gg_ref, gf_ref, gbw_ref,
                    aq_ref, ak_ref, av_ref, *, rope):
    x = x_ref[0]
    ms = jnp.mean(x * x, axis=-1, keepdims=True)
    h = x * lax.rsqrt(ms + NORM_EPS) * g_ref[...]
    h = h * (1.0 + mod_ref[0, 1:2, :]) + mod_ref[0, 0:1, :]
    hb = h.astype(BF16)
    p = _dot(hb, wn_ref[...])
    pt = _dot_nt(wt_ref[...], hb)

    rq = p[:, 0:128]
    rk = p[:, 128:256]
    gq = p[:, 768:896]
    if rope:
        cr = cr_ref[...]
        sr = sr_ref[...]
        rq = _rope_nat(rq, cr, sr, RET_DK // 4)
        rk = _rope_nat(rk, cr, sr, RET_DK // 4)
    rq_ref[0] = rq
    rk_ref[0] = rk * (RET_DK ** -0.5)
    rv_ref[0] = p[:, 256:512]
    rg_ref[0] = _silu(p[:, 512:768])
    gq_ref[0] = gq * (GLA_DK ** -0.5)
    gk_ref[0] = p[:, 896:1024]
    gv_ref[0] = p[:, 1024:1280]
    gg_ref[0] = _silu(p[:, 1280:1536])
    z = _dot(p[:, 1536:1664].astype(BF16), gw_ref[...]) + gb_ref[...]
    gates = _log_sigmoid(z) * (1.0 / GLA_TAU)
    gf_ref[0] = gates[:, 0:128]
    gbw_ref[0] = gates[:, 128:256]

    ct = ct_ref[...]
    st = st_ref[...]
    q3 = _head_norm_rope_t(pt[0:512], qg_ref[...], ct, st, rope)
    aq_ref[0] = (q3 * (GQA_HEAD_DIM ** -0.5 * LOG2_E)).astype(BF16)
    k3 = _head_norm_rope_t(pt[512:640], kg_ref[...], ct, st, rope)
    ak_ref[0] = k3.reshape(GQA_KV_HEADS * GQA_HEAD_DIM, k3.shape[2]).T.astype(BF16)
    av_ref[0] = pt[640:768].reshape(GQA_KV_HEADS, GQA_HEAD_DIM, pt.shape[1]).astype(BF16)


def _in_proj(x, mod, norm_g, w_nat, w_tr, gate_w_bd, gate_b_cat, q_gain, k_gain, tables, *, rope, tm):
    bsz, n, d = x.shape
    tm = min(tm, n)
    cos_r, sin_r, cos_t, sin_t = tables
    tok = lambda w: pl.BlockSpec((1, tm, w), lambda b, i: (b, i, 0))
    full = lambda a: pl.BlockSpec(a.shape, lambda b, i: (0,) * a.ndim)
    f32_out = lambda w: jax.ShapeDtypeStruct((bsz, n, w), F32)
    outs = pl.pallas_call(
        functools.partial(_in_proj_kernel, rope=rope),
        grid=(bsz, n // tm),
        in_specs=[tok(d),
                  pl.BlockSpec((1, 6, d), lambda b, i: (b, 0, 0)),
                  full(norm_g), full(w_nat), full(w_tr), full(gate_w_bd), full(gate_b_cat),
                  full(q_gain), full(k_gain),
                  pl.BlockSpec((tm, 128), lambda b, i: (i, 0)),
                  pl.BlockSpec((tm, 128), lambda b, i: (i, 0)),
                  pl.BlockSpec((GQA_HEAD_DIM, tm), lambda b, i: (0, i)),
                  pl.BlockSpec((GQA_HEAD_DIM, tm), lambda b, i: (0, i))],
        out_specs=[tok(128), tok(128), tok(256), tok(256), tok(128), tok(128), tok(256), tok(256),
                   tok(128), tok(128),
                   pl.BlockSpec((1, GQA_HEADS, GQA_HEAD_DIM, tm), lambda b, i: (b, 0, 0, i)),
                   tok(128),
                   pl.BlockSpec((1, GQA_KV_HEADS, GQA_HEAD_DIM, tm), lambda b, i: (b, 0, 0, i))],
        out_shape=[f32_out(128), f32_out(128), f32_out(256), f32_out(256),
                   f32_out(128), f32_out(128), f32_out(256), f32_out(256), f32_out(128), f32_out(128),
                   jax.ShapeDtypeStruct((bsz, GQA_HEADS, GQA_HEAD_DIM, n), BF16),
                   jax.ShapeDtypeStruct((bsz, n, 128), BF16),
                   jax.ShapeDtypeStruct((bsz, GQA_KV_HEADS, GQA_HEAD_DIM, n), BF16)],
        compiler_params=pltpu.CompilerParams(dimension_semantics=("parallel", "parallel"),
                                             vmem_limit_bytes=V7X_VMEM_LIMIT_BYTES),
        name="in_proj_rope" if rope else "in_proj_ctx",
    )(x, mod, norm_g, w_nat, w_tr, gate_w_bd, gate_b_cat, q_gain, k_gain, cos_r, sin_r, cos_t, sin_t)
    return outs


def _cumsum_rows(g, reverse):
    c = g.shape[0]
    row = lax.broadcasted_iota(jnp.int32, g.shape, 0)
    b = g
    s = 1
    while s < c:
        if reverse:
            b = b + jnp.where(row < c - s, pltpu.roll(b, c - s, 0), 0.0)
        else:
            b = b + jnp.where(row >= s, pltpu.roll(b, s, 0), 0.0)
        s *= 2
    return b


def _scan_chunk_local(q, k, v, b, b_end, reverse):
    c = q.shape[0]
    qd = (q * jnp.exp(b)).astype(BF16)
    kd = (k * jnp.exp(-b)).astype(BF16)
    ku = (k * jnp.exp(b_end - b)).astype(BF16)
    vb = v.astype(BF16)

    lane_k = lax.broadcasted_iota(jnp.int32, (c, 128), 1) // RET_DK
    lane_v = lax.broadcasted_iota(jnp.int32, (c, 256), 1) // RET_DV
    zk = jnp.zeros_like(kd)
    zv = jnp.zeros_like(vb)
    kd_stack = jnp.concatenate([jnp.where(lane_k == h, kd, zk) for h in range(RET_HEADS)], axis=0)
    v_stack = jnp.concatenate([jnp.where(lane_v == h, vb, zv) for h in range(RET_HEADS)], axis=0)

    a = _dot_nt(qd, kd_stack)
    u = _dot_tn(vb, ku)
    r = lax.broadcasted_iota(jnp.int32, u.shape, 0) // RET_DV
    l = lax.broadcasted_iota(jnp.int32, u.shape, 1) // RET_DK
    return qd, a, v_stack, jnp.where(r == l, u, 0.0), jnp.exp(b_end)


def _scan_chunk_intra(a, v_stack, reverse):
    c = a.shape[0]
    i = lax.broadcasted_iota(jnp.int32, a.shape, 0)
    j = lax.broadcasted_iota(jnp.int32, a.shape, 1) % c
    keep = (j > i) if reverse else (j <= i)
    return _dot(jnp.where(keep, a, 0.0).astype(BF16), v_stack)


def _scan_tile(fwd_refs, bwd_refs, out_refs, st_ref, lg_ref, *, gated, tile):
    c = SCAN_CHUNK
    n_chunks = tile // c
    order = {0: list(range(n_chunks)), 1: list(range(n_chunks - 1, -1, -1))}
    local = {}
    for direction in range(2):
        reverse = direction == 1
        q_ref, k_ref, v_ref, g_ref = bwd_refs if reverse else fwd_refs
        for ci in order[direction]:
            rows = pl.ds(ci * c, c)
            if gated:
                b = _cumsum_rows(g_ref[0, rows, :], reverse)
                b_end = b[0:1, :] if reverse else b[c - 1:c, :]
            else:
                lg = _log_sigmoid(lg_ref[direction:direction + 1, :])
                row = lax.broadcasted_iota(jnp.int32, (c, 128), 0)
                steps = (c - row) if reverse else (row + 1)
                b = steps.astype(F32) * lg
                b_end = float(c) * lg
            local[direction, ci] = _scan_chunk_local(q_ref[0, rows, :], k_ref[0, rows, :], v_ref[0, rows, :],
                                                     b, b_end, reverse)

    for direction in range(2):
        for ci in order[direction]:
            qd, a, v_stack, u, decay = local[direction, ci]
            out_refs[direction][0, pl.ds(ci * c, c), :] = _scan_chunk_intra(a, v_stack, direction == 1)
            local[direction, ci] = (qd, u, decay)

    start_state = {}
    for direction in range(2):
        st = st_ref[direction]
        for ci in order[direction]:
            _, u, decay = local[direction, ci]
            start_state[direction, ci] = st.astype(BF16)
            st = st * decay + u
        st_ref[direction] = st

    for direction in range(2):
        o_ref = out_refs[direction]
        for ci in order[direction]:
            rows = pl.ds(ci * c, c)
            o_ref[0, rows, :] = o_ref[0, rows, :] + _dot_nt(local[direction, ci][0], start_state[direction, ci])


def _scan_kernel(*refs, gated, tile):
    if gated:
        (qf, kf, vf, gf, qb, kb, vb, gb, qc, kc, vc, gfc, gbc, of, ob, ocf, ocb, st_ref) = refs
        lg_ref = None
    else:
        (qf, kf, vf, qb, kb, vb, qc, kc, vc, lg_ref, of, ob, ocf, ocb, st_ref) = refs
        gf = gb = gfc = gbc = None
    step = pl.program_id(1)
    tile_fn = functools.partial(_scan_tile, st_ref=st_ref, lg_ref=lg_ref, gated=gated, tile=tile)

    @pl.when(step == 0)
    def _():
        st_ref[...] = jnp.zeros_like(st_ref)
        tile_fn((qc, kc, vc, gfc), (qc, kc, vc, gbc), (ocf, ocb))

    @pl.when(step > 0)
    def _():
        tile_fn((qf, kf, vf, gf), (qb, kb, vb, gb), (of, ob))


def _bidir_scan(lat, ctx, gates_lat, gates_ctx, decay_logit, *, tile):
    q, k, v = lat
    bsz, n_lat, _ = q.shape
    n_lat_tiles = n_lat // tile
    assert ctx[0].shape[1] == tile, "context must be exactly one scan tile"
    fwd = lambda b, s: (b, jnp.maximum(s - 1, 0), 0)
    bwd = lambda b, s: (b, n_lat_tiles - 1 - jnp.maximum(s - 1, 0), 0)
    one = lambda b, s: (b, 0, 0)
    spec = lambda w, im: pl.BlockSpec((1, tile, w), im)
    qkv = lambda im: [spec(128, im), spec(128, im), spec(256, im)]
    gated = gates_lat is not None
    if gated:
        args = (q, k, v, gates_lat[0], q, k, v, gates_lat[1], *ctx, *gates_ctx)
        in_specs = (qkv(fwd) + [spec(128, fwd)] + qkv(bwd) + [spec(128, bwd)] + qkv(one)
                    + [spec(128, one), spec(128, one)])
    else:
        args = (q, k, v, q, k, v, *ctx, decay_logit)
        in_specs = qkv(fwd) + qkv(bwd) + qkv(one) + [pl.BlockSpec(decay_logit.shape, lambda b, s: (0, 0))]
    return pl.pallas_call(
        functools.partial(_scan_kernel, gated=gated, tile=tile),
        grid=(bsz, n_lat_tiles + 1),
        in_specs=in_specs,
        out_specs=[spec(256, fwd), spec(256, bwd), spec(256, one), spec(256, one)],
        out_shape=([jax.ShapeDtypeStruct((bsz, n_lat, 256), F32)] * 2
                   + [jax.ShapeDtypeStruct((bsz, tile, 256), F32)] * 2),
        scratch_shapes=[pltpu.VMEM((2, 256, 128), F32)],
        compiler_params=pltpu.CompilerParams(dimension_semantics=("parallel", "arbitrary"),
                                             vmem_limit_bytes=V7X_VMEM_LIMIT_BYTES),
        name="scan_gla" if gated else "scan_ret",
    )(*args)


def _attn_kernel(q_ref, k_ref, v_ref, o_ref, acc_ref, qpad_ref, *, n_kv_tiles):
    kv = pl.program_id(1)
    tq = q_ref.shape[3]
    tk = k_ref.shape[2]
    width = GQA_GROUP * tq
    qcat = jnp.concatenate([q_ref[0, g] for g in range(GQA_GROUP)], axis=1)
    zero = jnp.zeros_like(qcat)
    qpad_ref[...] = jnp.where(kv == 0, jnp.concatenate([qcat, zero], axis=0),
                              jnp.concatenate([zero, qcat], axis=0))
    acc_ref[...] = jnp.zeros_like(acc_ref)

    def body(j, m_all):
        n_cb = width // ATTN_COL_BLOCK
        m_parts = [m_all[:, cb * ATTN_COL_BLOCK:(cb + 1) * ATTN_COL_BLOCK] for cb in range(n_cb)]
        n_pairs = tk // (2 * ATTN_SUB_KEYS)
        units = [(2 * pr + h, cb) for pr in range(n_pairs) for cb in range(n_cb) for h in range(2)]
        rows = lambda sub, n=1: slice(sub * ATTN_SUB_KEYS, (sub + n) * ATTN_SUB_KEYS)
        cols = lambda cb: slice(cb * ATTN_COL_BLOCK, (cb + 1) * ATTN_COL_BLOCK)

        def scores(u):
            sub, cb = units[u]
            return _dot(k_ref[0, j, rows(sub), :], qpad_ref[:, cols(cb)])

        pending = [scores(u) for u in range(min(ATTN_LOOKAHEAD, len(units)))]
        for u in range(0, len(units), 2):
            sub, cb = units[u]
            for ahead in (u + ATTN_LOOKAHEAD, u + ATTN_LOOKAHEAD + 1):
                if ahead < len(units):
                    pending.append(scores(ahead))
            s_a = pending.pop(0).astype(BF16)
            s_b = pending.pop(0).astype(BF16)
            m_a = jnp.maximum(m_parts[cb], jnp.max(s_a, axis=0, keepdims=True))
            p_a = jnp.exp2(s_a - m_a)
            m_b = jnp.maximum(m_a, jnp.max(s_b, axis=0, keepdims=True))
            p_b = jnp.exp2(s_b - m_b)
            p = jnp.concatenate([p_a * jnp.exp2(m_a - m_b), p_b], axis=0)
            alpha = jnp.exp2(m_parts[cb].astype(F32) - m_b.astype(F32))
            acc_ref[:, cols(cb)] = acc_ref[:, cols(cb)] * alpha + _dot(v_ref[0, 0, j, :, rows(sub, 2)], p)
            m_parts[cb] = m_b
        return jnp.concatenate(m_parts, axis=1)

    lax.fori_loop(0, n_kv_tiles, body, jnp.full((1, width), -jnp.inf, BF16))

    acc = acc_ref[...]
    o = acc[0:GQA_HEAD_DIM] * (1.0 / acc[GQA_HEAD_DIM:GQA_HEAD_DIM + 1])
    o_ref[0] = jnp.concatenate([o[:, g * tq:(g + 1) * tq].T for g in range(GQA_GROUP)], axis=1).astype(o_ref.dtype)


def _attention(q_t, k_nat, v_t, *, tq, tk):
    bsz, _, _, nq = q_t.shape
    ns = k_nat.shape[1]
    tq = min(tq, nq)
    tk = min(tk, ns)
    n_kv_tiles = ns // tk
    k_tiles = k_nat.reshape(bsz, n_kv_tiles, tk, 128)
    ones = jnp.ones((bsz, GQA_KV_HEADS, V_ROWS - GQA_HEAD_DIM, ns), BF16)
    v_tiles = jnp.concatenate([v_t, ones], axis=2).reshape(bsz, GQA_KV_HEADS, V_ROWS, n_kv_tiles, tk)
    v_tiles = v_tiles.transpose(0, 1, 3, 2, 4)
    return pl.pallas_call(
        functools.partial(_attn_kernel, n_kv_tiles=n_kv_tiles),
        grid=(bsz, GQA_KV_HEADS, nq // tq),
        in_specs=[pl.BlockSpec((1, GQA_GROUP, GQA_HEAD_DIM, tq), lambda b, h, i: (b, h, 0, i)),
                  pl.BlockSpec((1, n_kv_tiles, tk, 128), lambda b, h, i: (b, 0, 0, 0)),
                  pl.BlockSpec((1, 1, n_kv_tiles, V_ROWS, tk), lambda b, h, i: (b, h, 0, 0, 0))],
        out_specs=pl.BlockSpec((1, tq, GQA_GROUP * GQA_HEAD_DIM), lambda b, h, i: (b, i, h)),
        out_shape=jax.ShapeDtypeStruct((bsz, nq, GQA_HEADS * GQA_HEAD_DIM), BF16),
        scratch_shapes=[pltpu.VMEM((V_ROWS, GQA_GROUP * tq), F32),
                        pltpu.VMEM((2 * GQA_HEAD_DIM, GQA_GROUP * tq), BF16)],
        compiler_params=pltpu.CompilerParams(dimension_semantics=("parallel", "parallel", "parallel"),
                                             vmem_limit_bytes=V7X_VMEM_LIMIT_BYTES),
        name="gqa_attention",
    )(q_t, k_tiles, v_tiles)


def _group_mean(x, gmat):
    hi = x.astype(BF16)
    lo = (x - hi.astype(F32)).astype(BF16)
    return (_dot(hi, gmat) + _dot(lo, gmat)) * (1.0 / RET_DV)


def _post_kernel(x_ref, mod_ref, rf_ref, rb_ref, rg_ref, gf_ref, gb_ref, gg_ref, ya_ref, wo_ref,
                 g_ref, w1_ref, w2_ref, fg_ref, o_ref, *, ff_tile, final_norm):
    r = lax.broadcasted_iota(jnp.int32, (256, 256), 0) // RET_DV
    c = lax.broadcasted_iota(jnp.int32, (256, 256), 1) // RET_DV
    gmat = jnp.where(r == c, 1.0, 0.0).astype(BF16)
    ro = rf_ref[0] + rb_ref[0]
    rc = ro - _group_mean(ro, gmat)
    y_ret = rg_ref[0] * (rc * lax.rsqrt(_group_mean(rc * rc, gmat) + NORM_EPS))
    go = gf_ref[0] + gb_ref[0]
    y_gla = gg_ref[0] * (go * lax.rsqrt(_group_mean(go * go, gmat) + NORM_EPS))
    y = jnp.concatenate([y_ret.astype(BF16), y_gla.astype(BF16), ya_ref[0]], axis=1)
    x = x_ref[0] + mod_ref[0, 2:3, :] * _dot(y, wo_ref[...])

    ms = jnp.mean(x * x, axis=-1, keepdims=True)
    h = x * lax.rsqrt(ms + NORM_EPS) * g_ref[...]
    hb = (h * (1.0 + mod_ref[0, 4:5, :]) + mod_ref[0, 3:4, :]).astype(BF16)
    d_ff = w1_ref.shape[1]
    acc = jnp.zeros(x.shape, F32)
    for c in range(d_ff // ff_tile):
        cols = slice(c * ff_tile, (c + 1) * ff_tile)
        a = jnp.maximum(_dot(hb, w1_ref[:, cols]), 0.0)
        acc = acc + _dot((a * a).astype(BF16), w2_ref[cols, :])
    y = x + mod_ref[0, 5:6, :] * acc
    if final_norm:
        y = y * lax.rsqrt(jnp.mean(y * y, axis=-1, keepdims=True) + NORM_EPS) * fg_ref[...]
    o_ref[0] = y


def _post(x, mod, scans, gates, y_attn, w_out, norm_g, w1, w2, final_g, *, tm, final_norm):
    bsz, n, d = x.shape
    tm = min(tm, n)
    tok = lambda w: pl.BlockSpec((1, tm, w), lambda b, i: (b, i, 0))
    full = lambda a: pl.BlockSpec(a.shape, lambda b, i: (0,) * a.ndim, pipeline_mode=pl.Buffered(1))
    rf, rb, gf, gb = scans
    rg, gg = gates
    return pl.pallas_call(
        functools.partial(_post_kernel, ff_tile=1024, final_norm=final_norm),
        grid=(bsz, n // tm),
        in_specs=[tok(d), pl.BlockSpec((1, 6, d), lambda b, i: (b, 0, 0)),
                  tok(256), tok(256), tok(256), tok(256), tok(256), tok(256), tok(512),
                  full(w_out), full(norm_g), full(w1), full(w2), full(final_g)],
        out_specs=tok(d),
        out_shape=jax.ShapeDtypeStruct((bsz, n, d), F32),
        compiler_params=pltpu.CompilerParams(dimension_semantics=("parallel", "parallel"),
                                             vmem_limit_bytes=V7X_VMEM_LIMIT_BYTES),
        name="post_final" if final_norm else "post",
    )(x, mod, rf, rb, rg, gf, gb, gg, y_attn, w_out, norm_g, w1, w2, final_g)


def _prep_w_in(w):
    d = w.shape[0]
    ret = w[:, 0:768]
    gla = w[:, 768:1536]
    low = w[:, 1536:1568]
    aq = w[:, 1568:2080]
    ak = w[:, 2080:2208]
    av = w[:, 2208:2336]
    w_nat = jnp.concatenate([ret, gla, low, jnp.zeros((d, 128 - 2 * GLA_RANK), w.dtype)], axis=1)
    w_tr = jnp.concatenate([aq, ak, av], axis=1).T
    return w_nat.astype(BF16), w_tr.astype(BF16)


def _rope_tables(n_lat):
    t = jnp.arange(n_lat)
    row = (t // GRID_W).astype(F32)
    col = (t % GRID_W).astype(F32)

    def head_tables(head_dim):
        nf = head_dim // 4
        inv_freq = ROPE_BASE ** (-jnp.arange(nf, dtype=F32) / nf)
        ang_r = row[:, None] * inv_freq
        ang_c = col[:, None] * inv_freq
        cos = jnp.concatenate([jnp.cos(ang_r)] * 2 + [jnp.cos(ang_c)] * 2, axis=1)
        sin = jnp.concatenate([-jnp.sin(ang_r), jnp.sin(ang_r), -jnp.sin(ang_c), jnp.sin(ang_c)], axis=1)
        return cos, sin

    cos_r, sin_r = head_tables(RET_DK)
    cos_r = jnp.tile(cos_r, (1, RET_HEADS))
    sin_r = jnp.tile(sin_r, (1, RET_HEADS))
    cos_a, sin_a = head_tables(GQA_HEAD_DIM)
    return cos_r, sin_r, cos_a.T, sin_a.T


def kernel(x, c, ctx, c_ctx, mod_w, mod_b, attn_norm_g, mlp_norm_g, w_in, w_out, ret_decay_logit,
           gla_gate_w, gla_gate_b, qk_norm_g, mlp_w1, mlp_w2, final_norm_g):
    bsz, n_lat, d = x.shape
    n_ctx = ctx.shape[1]
    depth = mod_w.shape[0]
    scan_tile = n_ctx
    assert n_lat % scan_tile == 0 and scan_tile % SCAN_CHUNK == 0 and bsz <= 7

    cvec = jnp.zeros((8, d), F32).at[:bsz].set(c).at[bsz].set(c_ctx)
    mods = _modulation(cvec, mod_w, mod_b)
    lat_tables = _rope_tables(n_lat)
    ctx_tables = tuple(jnp.zeros((n_ctx, 128), F32) for _ in range(2)) + \
        tuple(jnp.zeros((GQA_HEAD_DIM, n_ctx), F32) for _ in range(2))
    final_g = final_norm_g.reshape(1, d)

    for i in range(depth):
        need_ctx = i < depth - 1
        mod_lat = mods[i, :bsz].reshape(bsz, 6, d)
        mod_ctx = jnp.broadcast_to(mods[i, bsz].reshape(1, 6, d), (bsz, 6, d))
        w_nat, w_tr = _prep_w_in(w_in[i])
        gw = jnp.zeros((128, 256), F32)
        gw = gw.at[0:GLA_RANK, 0:128].set(gla_gate_w[i, 0]).at[GLA_RANK:2 * GLA_RANK, 128:256].set(gla_gate_w[i, 1])
        gw = gw.astype(BF16)
        gb = gla_gate_b[i].reshape(1, 256)
        q_gain = qk_norm_g[i, 0].reshape(GQA_HEAD_DIM, 1)
        k_gain = qk_norm_g[i, 1].reshape(GQA_HEAD_DIM, 1)
        attn_g = attn_norm_g[i].reshape(1, d)
        mlp_g = mlp_norm_g[i].reshape(1, d)
        decay = jnp.repeat(ret_decay_logit[i], RET_DK, axis=1)
        w_o = w_out[i].astype(BF16)
        w1 = mlp_w1[i].astype(BF16)
        w2 = mlp_w2[i].astype(BF16)

        pl_lat = _in_proj(x, mod_lat, attn_g, w_nat, w_tr, gw, gb, q_gain, k_gain, lat_tables, rope=True, tm=512)
        pl_ctx = _in_proj(ctx, mod_ctx, attn_g, w_nat, w_tr, gw, gb, q_gain, k_gain, ctx_tables, rope=False,
                          tm=n_ctx)
        ret = _bidir_scan(pl_lat[0:3], pl_ctx[0:3], None, None, decay, tile=scan_tile)
        gla = _bidir_scan(pl_lat[4:7], pl_ctx[4:7], pl_lat[8:10], pl_ctx[8:10], None, tile=scan_tile)
        ak = jnp.concatenate([pl_lat[11], pl_ctx[11]], axis=1)
        av = jnp.concatenate([pl_lat[12], pl_ctx[12]], axis=3)
        y_attn = _attention(pl_lat[10], ak, av, tq=256, tk=8320)

        x = _post(x, mod_lat, (ret[0], ret[1], gla[0], gla[1]), (pl_lat[3], pl_lat[7]), y_attn, w_o,
                  mlp_g, w1, w2, final_g, tm=512, final_norm=not need_ctx)
        if need_ctx:
            y_attn_c = _attention(pl_ctx[10], pl_ctx[11], pl_ctx[12], tq=256, tk=256)
            ctx = _post(ctx, mod_ctx, (ret[2], ret[3], gla[2], gla[3]), (pl_ctx[3], pl_ctx[7]), y_attn_c, w_o,
                        mlp_g, w1, w2, final_g, tm=n_ctx, final_norm=False)
    return x
```

```python
import functools

import jax
import jax.numpy as jnp
import numpy as np
from jax import lax
from jax.experimental import pallas as pl
from jax.experimental.pallas import tpu as pltpu

F32 = jnp.float32
BF16 = jnp.bfloat16

GRID_W = 64
RET_HEADS = 4
RET_DK = 32
RET_DV = 64
GLA_HEADS = 4
GLA_DK = 32
GLA_DV = 64
GLA_RANK = 16
GLA_TAU = 16.0
GQA_HEADS = 8
GQA_KV_HEADS = 2
GQA_GROUP = 4
GQA_HEAD_DIM = 64
SCAN_CHUNK = 64
ROPE_BASE = 10000.0
NORM_EPS = 1e-6

V7X_VMEM_LIMIT_BYTES = 56 * 1024 * 1024
V_ROWS = 80
ATTN_SUB_KEYS = 128
ATTN_COL_BLOCK = 256
ATTN_LOOKAHEAD = 8
LOG2_E = 1.4426950408889634

NAT_WIDTH = 1664
TR_WIDTH = 768


def _silu(x):
    return x * (1.0 / (1.0 + jnp.exp(-x)))


def _log_sigmoid(z):
    return jnp.minimum(z, 0.0) - jnp.log(1.0 + jnp.exp(-jnp.abs(z)))


def _dot(a, b):
    return jnp.dot(a, b, preferred_element_type=F32)


def _dot_nt(a, b):
    return lax.dot_general(a, b, (((1,), (1,)), ((), ())), preferred_element_type=F32)


def _dot_tn(a, b):
    return lax.dot_general(a, b, (((0,), (0,)), ((), ())), preferred_element_type=F32)


def _mod_kernel(c_ref, w_ref, b_ref, o_ref):
    s = _silu(c_ref[...]).astype(BF16)
    o_ref[0] = _dot(s, w_ref[0].astype(BF16)) + b_ref[0]


def _modulation(cvec, mod_w, mod_b):
    depth, d, n = mod_w.shape
    tn = 1536
    return pl.pallas_call(
        _mod_kernel,
        grid=(depth, n // tn),
        in_specs=[pl.BlockSpec((8, d), lambda l, j: (0, 0)),
                  pl.BlockSpec((1, d, tn), lambda l, j: (l, 0, j)),
                  pl.BlockSpec((1, 1, tn), lambda l, j: (l, 0, j))],
        out_specs=pl.BlockSpec((1, 8, tn), lambda l, j: (l, 0, j)),
        out_shape=jax.ShapeDtypeStruct((depth, 8, n), F32),
        compiler_params=pltpu.CompilerParams(dimension_semantics=("parallel", "parallel"),
                                             vmem_limit_bytes=V7X_VMEM_LIMIT_BYTES),
        name="modulation",
    )(cvec, mod_w, mod_b.reshape(depth, 1, n))


def _rope_nat(x, cos, sin_signed, nf):
    lane = lax.broadcasted_iota(jnp.int32, x.shape, 1)
    first = (lane % (2 * nf)) < nf
    partner = jnp.where(first, pltpu.roll(x, 128 - nf, 1), pltpu.roll(x, nf, 1))
    return x * cos + partner * sin_signed


def _head_norm_rope_t(xt, g_col, cos_t, sin_t, rope):
    h = xt.shape[0] // GQA_HEAD_DIM
    x3 = xt.reshape(h, GQA_HEAD_DIM, xt.shape[1])
    ms = jnp.sum(x3 * x3, axis=1, keepdims=True) * (1.0 / GQA_HEAD_DIM)
    x3 = x3 * lax.rsqrt(ms + NORM_EPS) * g_col[None]
    if rope:
        swapped = jnp.concatenate([x3[:, 16:32], x3[:, 0:16], x3[:, 48:64], x3[:, 32:48]], axis=1)
        x3 = x3 * cos_t[None] + swapped * sin_t[None]
    return x3


def _in_proj_kernel(x_ref, mod_ref, g_ref, wn_ref, wt_ref, gw_ref, gb_ref, qg_ref, kg_ref,
                    cr_ref, sr_ref, ct_ref, st_ref,
                    rq_ref, rk_ref, rv_ref, rg_ref, gq_ref, gk_ref, gv_ref, gg_ref, gf_ref, gbw_ref,
                    aq_ref, ak_ref, av_ref, *, rope):
    x = x_ref[0]
    ms = jnp.mean(x * x, axis=-1, keepdims=True)
    h = x * lax.rsqrt(ms + NORM_EPS) * g_ref[...]
    h = h * (1.0 + mod_ref[0, 1:2, :]) + mod_ref[0, 0:1, :]
    hb = h.astype(BF16)
    p = _dot(hb, wn_ref[...])
    pt = _dot_nt(wt_ref[...], hb)

    rq = p[:, 0:128]
    rk = p[:, 128:256]
    gq = p[:, 768:896]
    if rope:
        cr = cr_ref[...]
        sr = sr_ref[...]
        rq = _rope_nat(rq, cr, sr, RET_DK // 4)
        rk = _rope_nat(rk, cr, sr, RET_DK // 4)
    rq_ref[0] = rq
    rk_ref[0] = rk * (RET_DK ** -0.5)
    rv_ref[0] = p[:, 256:512].astype(BF16)
    rg_ref[0] = _silu(p[:, 512:768])
    gq_ref[0] = gq * (GLA_DK ** -0.5)
    gk_ref[0] = p[:, 896:1024]
    gv_ref[0] = p[:, 1024:1280].astype(BF16)
    gg_ref[0] = _silu(p[:, 1280:1536])
    z = _dot(p[:, 1536:1664].astype(BF16), gw_ref[...]) + gb_ref[...]
    gates = _log_sigmoid(z) * (1.0 / GLA_TAU)
    gf_ref[0] = gates[:, 0:128]
    gbw_ref[0] = gates[:, 128:256]

    ct = ct_ref[...]
    st = st_ref[...]
    q3 = _head_norm_rope_t(pt[0:512], qg_ref[...], ct, st, rope)
    aq_ref[0] = (q3 * (GQA_HEAD_DIM ** -0.5 * LOG2_E)).astype(BF16)
    k3 = _head_norm_rope_t(pt[512:640], kg_ref[...], ct, st, rope)
    ak_ref[0] = k3.reshape(GQA_KV_HEADS * GQA_HEAD_DIM, k3.shape[2]).T.astype(BF16)
    av_ref[0] = pt[640:768].reshape(GQA_KV_HEADS, GQA_HEAD_DIM, pt.shape[1]).astype(BF16)


def _in_proj(x, mod, norm_g, w_nat, w_tr, gate_w_bd, gate_b_cat, q_gain, k_gain, tables, *, rope, tm):
    bsz, n, d = x.shape
    tm = min(tm, n)
    cos_r, sin_r, cos_t, sin_t = tables
    tok = lambda w: pl.BlockSpec((1, tm, w), lambda b, i: (b, i, 0))
    full = lambda a: pl.BlockSpec(a.shape, lambda b, i: (0,) * a.ndim)
    f32_out = lambda w: jax.ShapeDtypeStruct((bsz, n, w), F32)
    outs = pl.pallas_call(
        functools.partial(_in_proj_kernel, rope=rope),
        grid=(bsz, n // tm),
        in_specs=[tok(d),
                  pl.BlockSpec((1, 6, d), lambda b, i: (b, 0, 0)),
                  full(norm_g), full(w_nat), full(w_tr), full(gate_w_bd), full(gate_b_cat),
                  full(q_gain), full(k_gain),
                  pl.BlockSpec((tm, 128), lambda b, i: (i, 0)),
                  pl.BlockSpec((tm, 128), lambda b, i: (i, 0)),
                  pl.BlockSpec((GQA_HEAD_DIM, tm), lambda b, i: (0, i)),
                  pl.BlockSpec((GQA_HEAD_DIM, tm), lambda b, i: (0, i))],
        out_specs=[tok(128), tok(128), tok(256), tok(256), tok(128), tok(128), tok(256), tok(256),
                   tok(128), tok(128),
                   pl.BlockSpec((1, GQA_HEADS, GQA_HEAD_DIM, tm), lambda b, i: (b, 0, 0, i)),
                   tok(128),
                   pl.BlockSpec((1, GQA_KV_HEADS, GQA_HEAD_DIM, tm), lambda b, i: (b, 0, 0, i))],
        out_shape=[f32_out(128), f32_out(128), jax.ShapeDtypeStruct((bsz, n, 256), BF16), f32_out(256),
                   f32_out(128), f32_out(128), jax.ShapeDtypeStruct((bsz, n, 256), BF16), f32_out(256),
                   f32_out(128), f32_out(128),
                   jax.ShapeDtypeStruct((bsz, GQA_HEADS, GQA_HEAD_DIM, n), BF16),
                   jax.ShapeDtypeStruct((bsz, n, 128), BF16),
                   jax.ShapeDtypeStruct((bsz, GQA_KV_HEADS, GQA_HEAD_DIM, n), BF16)],
        compiler_params=pltpu.CompilerParams(dimension_semantics=("parallel", "parallel"),
                                             vmem_limit_bytes=V7X_VMEM_LIMIT_BYTES),
        name="in_proj_rope" if rope else "in_proj_ctx",
    )(x, mod, norm_g, w_nat, w_tr, gate_w_bd, gate_b_cat, q_gain, k_gain, cos_r, sin_r, cos_t, sin_t)
    return outs


def _cumsum_rows(g, reverse):
    c = g.shape[0]
    row = lax.broadcasted_iota(jnp.int32, g.shape, 0)
    b = g
    s = 1
    while s < c:
        if reverse:
            b = b + jnp.where(row < c - s, pltpu.roll(b, c - s, 0), 0.0)
        else:
            b = b + jnp.where(row >= s, pltpu.roll(b, s, 0), 0.0)
        s *= 2
    return b


def _scan_chunk_local(q, k, v, b, b_end, reverse):
    c = q.shape[0]
    qd = (q * jnp.exp(b)).astype(BF16)
    kd = (k * jnp.exp(-b)).astype(BF16)
    ku = (k * jnp.exp(b_end - b)).astype(BF16)
    vb = v.astype(BF16)

    lane_k = lax.broadcasted_iota(jnp.int32, (c, 128), 1) // RET_DK
    lane_v = lax.broadcasted_iota(jnp.int32, (c, 256), 1) // RET_DV
    zk = jnp.zeros_like(kd)
    zv = jnp.zeros_like(vb)
    kd_stack = jnp.concatenate([jnp.where(lane_k == h, kd, zk) for h in range(RET_HEADS)], axis=0)
    v_stack = jnp.concatenate([jnp.where(lane_v == h, vb, zv) for h in range(RET_HEADS)], axis=0)

    a = _dot_nt(qd, kd_stack)
    u = _dot_tn(vb, ku)
    r = lax.broadcasted_iota(jnp.int32, u.shape, 0) // RET_DV
    l = lax.broadcasted_iota(jnp.int32, u.shape, 1) // RET_DK
    return qd, a, v_stack, jnp.where(r == l, u, 0.0), jnp.exp(b_end)


def _scan_chunk_intra(a, v_stack, reverse):
    c = a.shape[0]
    i = lax.broadcasted_iota(jnp.int32, a.shape, 0)
    j = lax.broadcasted_iota(jnp.int32, a.shape, 1) % c
    keep = (j > i) if reverse else (j <= i)
    return _dot(jnp.where(keep, a, 0.0).astype(BF16), v_stack)


def _scan_tile(fwd_refs, bwd_refs, out_refs, st_ref, lg_ref, *, gated, tile):
    c = SCAN_CHUNK
    n_chunks = tile // c
    order = {0: list(range(n_chunks)), 1: list(range(n_chunks - 1, -1, -1))}
    local = {}
    for direction in range(2):
        reverse = direction == 1
        q_ref, k_ref, v_ref, g_ref = bwd_refs if reverse else fwd_refs
        for ci in order[direction]:
            rows = pl.ds(ci * c, c)
            if gated:
                b = _cumsum_rows(g_ref[0, rows, :], reverse)
                b_end = b[0:1, :] if reverse else b[c - 1:c, :]
            else:
                lg = _log_sigmoid(lg_ref[direction:direction + 1, :])
                row = lax.broadcasted_iota(jnp.int32, (c, 128), 0)
                steps = (c - row) if reverse else (row + 1)
                b = steps.astype(F32) * lg
                b_end = float(c) * lg
            local[direction, ci] = _scan_chunk_local(q_ref[0, rows, :], k_ref[0, rows, :], v_ref[0, rows, :],
                                                     b, b_end, reverse)

    for direction in range(2):
        for ci in order[direction]:
            qd, a, v_stack, u, decay = local[direction, ci]
            out_refs[direction][0, pl.ds(ci * c, c), :] = _scan_chunk_intra(a, v_stack, direction == 1)
            local[direction, ci] = (qd, u, decay)

    start_state = {}
    for direction in range(2):
        st = st_ref[direction]
        for ci in order[direction]:
            _, u, decay = local[direction, ci]
            start_state[direction, ci] = st.astype(BF16)
            st = st * decay + u
        st_ref[direction] = st

    for direction in range(2):
        o_ref = out_refs[direction]
        for ci in order[direction]:
            rows = pl.ds(ci * c, c)
            o_ref[0, rows, :] = o_ref[0, rows, :] + _dot_nt(local[direction, ci][0], start_state[direction, ci])


def _scan_kernel(*refs, gated, tile, ctx_tile):
    if gated:
        (qf, kf, vf, gf, qb, kb, vb, gb, qc, kc, vc, gfc, gbc, of, ob, ocf, ocb, st_ref) = refs
        lg_ref = None
    else:
        (qf, kf, vf, qb, kb, vb, qc, kc, vc, lg_ref, of, ob, ocf, ocb, st_ref) = refs
        gf = gb = gfc = gbc = None
    step = pl.program_id(1)
    tile_fn = functools.partial(_scan_tile, st_ref=st_ref, lg_ref=lg_ref, gated=gated)

    @pl.when(step == 0)
    def _():
        st_ref[...] = jnp.zeros_like(st_ref)
        tile_fn((qc, kc, vc, gfc), (qc, kc, vc, gbc), (ocf, ocb), tile=ctx_tile)

    @pl.when(step > 0)
    def _():
        tile_fn((qf, kf, vf, gf), (qb, kb, vb, gb), (of, ob), tile=tile)


def _bidir_scan(lat, ctx, gates_lat, gates_ctx, decay_logit, *, tile):
    q, k, v = lat
    bsz, n_lat, _ = q.shape
    tile = min(tile, n_lat)
    n_lat_tiles = n_lat // tile
    ctx_tile = ctx[0].shape[1]
    fwd = lambda b, s: (b, jnp.maximum(s - 1, 0), 0)
    bwd = lambda b, s: (b, n_lat_tiles - 1 - jnp.maximum(s - 1, 0), 0)
    one = lambda b, s: (b, 0, 0)
    spec = lambda w, im: pl.BlockSpec((1, ctx_tile if im is one else tile, w), im)
    qkv = lambda im: [spec(128, im), spec(128, im), spec(256, im)]
    gated = gates_lat is not None
    if gated:
        args = (q, k, v, gates_lat[0], q, k, v, gates_lat[1], *ctx, *gates_ctx)
        in_specs = (qkv(fwd) + [spec(128, fwd)] + qkv(bwd) + [spec(128, bwd)] + qkv(one)
                    + [spec(128, one), spec(128, one)])
    else:
        args = (q, k, v, q, k, v, *ctx, decay_logit)
        in_specs = qkv(fwd) + qkv(bwd) + qkv(one) + [pl.BlockSpec(decay_logit.shape, lambda b, s: (0, 0))]
    return pl.pallas_call(
        functools.partial(_scan_kernel, gated=gated, tile=tile, ctx_tile=ctx_tile),
        grid=(bsz, n_lat_tiles + 1),
        in_specs=in_specs,
        out_specs=[spec(256, fwd), spec(256, bwd), spec(256, one), spec(256, one)],
        out_shape=([jax.ShapeDtypeStruct((bsz, n_lat, 256), F32)] * 2
                   + [jax.ShapeDtypeStruct((bsz, ctx_tile, 256), F32)] * 2),
        scratch_shapes=[pltpu.VMEM((2, 256, 128), F32)],
        compiler_params=pltpu.CompilerParams(dimension_semantics=("parallel", "arbitrary"),
                                             vmem_limit_bytes=V7X_VMEM_LIMIT_BYTES),
        name="scan_gla" if gated else "scan_ret",
    )(*args)


def _attn_kernel(q_ref, k_ref, v_ref, o_ref, acc_ref, qpad_ref, *, n_kv_tiles):
    kv = pl.program_id(1)
    tq = q_ref.shape[3]
    tk = k_ref.shape[2]
    width = GQA_GROUP * tq
    qcat = jnp.concatenate([q_ref[0, g] for g in range(GQA_GROUP)], axis=1)
    zero = jnp.zeros_like(qcat)
    qpad_ref[...] = jnp.where(kv == 0, jnp.concatenate([qcat, zero], axis=0),
                              jnp.concatenate([zero, qcat], axis=0))
    acc_ref[...] = jnp.zeros_like(acc_ref)

    def body(j, m_all):
        n_cb = width // ATTN_COL_BLOCK
        m_parts = [m_all[:, cb * ATTN_COL_BLOCK:(cb + 1) * ATTN_COL_BLOCK] for cb in range(n_cb)]
        n_pairs = tk // (2 * ATTN_SUB_KEYS)
        units = [(2 * pr + h, cb) for pr in range(n_pairs) for cb in range(n_cb) for h in range(2)]
        rows = lambda sub, n=1: slice(sub * ATTN_SUB_KEYS, (sub + n) * ATTN_SUB_KEYS)
        cols = lambda cb: slice(cb * ATTN_COL_BLOCK, (cb + 1) * ATTN_COL_BLOCK)

        def scores(u):
            sub, cb = units[u]
            return _dot(k_ref[0, j, rows(sub), :], qpad_ref[:, cols(cb)])

        pending = [scores(u) for u in range(min(ATTN_LOOKAHEAD, len(units)))]
        for u in range(0, len(units), 2):
            sub, cb = units[u]
            for ahead in (u + ATTN_LOOKAHEAD, u + ATTN_LOOKAHEAD + 1):
                if ahead < len(units):
                    pending.append(scores(ahead))
            s_a = pending.pop(0).astype(BF16)
            s_b = pending.pop(0).astype(BF16)
            m_a = jnp.maximum(m_parts[cb], jnp.max(s_a, axis=0, keepdims=True))
            p_a = jnp.exp2(s_a - m_a)
            m_b = jnp.maximum(m_a, jnp.max(s_b, axis=0, keepdims=True))
            p_b = jnp.exp2(s_b - m_b)
            p = jnp.concatenate([p_a * jnp.exp2(m_a - m_b), p_b], axis=0)
            alpha = jnp.exp2(m_parts[cb].astype(F32) - m_b.astype(F32))
            acc_ref[:, cols(cb)] = acc_ref[:, cols(cb)] * alpha + _dot(v_ref[0, 0, j, :, rows(sub, 2)], p)
            m_parts[cb] = m_b
        return jnp.concatenate(m_parts, axis=1)

    lax.fori_loop(0, n_kv_tiles, body, jnp.full((1, width), -jnp.inf, BF16))

    acc = acc_ref[...]
    o = acc[0:GQA_HEAD_DIM] * (1.0 / acc[GQA_HEAD_DIM:GQA_HEAD_DIM + 1])
    o_ref[0] = jnp.concatenate([o[:, g * tq:(g + 1) * tq].T for g in range(GQA_GROUP)], axis=1).astype(o_ref.dtype)


def _attention(q_t, k_nat, v_t, *, tq, tk):
    bsz, _, _, nq = q_t.shape
    ns = k_nat.shape[1]
    tq = min(tq, nq)
    tk = min(tk, ns)
    n_kv_tiles = ns // tk
    k_tiles = k_nat.reshape(bsz, n_kv_tiles, tk, 128)
    ones = jnp.ones((bsz, GQA_KV_HEADS, V_ROWS - GQA_HEAD_DIM, ns), BF16)
    v_tiles = jnp.concatenate([v_t, ones], axis=2).reshape(bsz, GQA_KV_HEADS, V_ROWS, n_kv_tiles, tk)
    v_tiles = v_tiles.transpose(0, 1, 3, 2, 4)
    return pl.pallas_call(
        functools.partial(_attn_kernel, n_kv_tiles=n_kv_tiles),
        grid=(bsz, GQA_KV_HEADS, nq // tq),
        in_specs=[pl.BlockSpec((1, GQA_GROUP, GQA_HEAD_DIM, tq), lambda b, h, i: (b, h, 0, i)),
                  pl.BlockSpec((1, n_kv_tiles, tk, 128), lambda b, h, i: (b, 0, 0, 0)),
                  pl.BlockSpec((1, 1, n_kv_tiles, V_ROWS, tk), lambda b, h, i: (b, h, 0, 0, 0))],
        out_specs=pl.BlockSpec((1, tq, GQA_GROUP * GQA_HEAD_DIM), lambda b, h, i: (b, i, h)),
        out_shape=jax.ShapeDtypeStruct((bsz, nq, GQA_HEADS * GQA_HEAD_DIM), BF16),
        scratch_shapes=[pltpu.VMEM((V_ROWS, GQA_GROUP * tq), F32),
                        pltpu.VMEM((2 * GQA_HEAD_DIM, GQA_GROUP * tq), BF16)],
        compiler_params=pltpu.CompilerParams(dimension_semantics=("parallel", "parallel", "parallel"),
                                             vmem_limit_bytes=V7X_VMEM_LIMIT_BYTES),
        name="gqa_attention",
    )(q_t, k_tiles, v_tiles)


def _group_mean(x, gmat):
    hi = x.astype(BF16)
    lo = (x - hi.astype(F32)).astype(BF16)
    return (_dot(hi, gmat) + _dot(lo, gmat)) * (1.0 / RET_DV)


def _post_kernel(x_ref, mod_ref, rf_ref, rb_ref, rg_ref, gf_ref, gb_ref, gg_ref, ya_ref, wo_ref,
                 g_ref, w1_ref, w2_ref, fg_ref, o_ref, *, ff_tile, final_norm):
    r = lax.broadcasted_iota(jnp.int32, (256, 256), 0) // RET_DV
    c = lax.broadcasted_iota(jnp.int32, (256, 256), 1) // RET_DV
    gmat = jnp.where(r == c, 1.0, 0.0).astype(BF16)
    ro = rf_ref[0] + rb_ref[0]
    rc = ro - _group_mean(ro, gmat)
    y_ret = rg_ref[0] * (rc * lax.rsqrt(_group_mean(rc * rc, gmat) + NORM_EPS))
    go = gf_ref[0] + gb_ref[0]
    y_gla = gg_ref[0] * (go * lax.rsqrt(_group_mean(go * go, gmat) + NORM_EPS))
    y = jnp.concatenate([y_ret.astype(BF16), y_gla.astype(BF16), ya_ref[0]], axis=1)
    x = x_ref[0] + mod_ref[0, 2:3, :] * _dot(y, wo_ref[...])

    ms = jnp.mean(x * x, axis=-1, keepdims=True)
    h = x * lax.rsqrt(ms + NORM_EPS) * g_ref[...]
    hb = (h * (1.0 + mod_ref[0, 4:5, :]) + mod_ref[0, 3:4, :]).astype(BF16)
    d_ff = w1_ref.shape[1]
    acc = jnp.zeros(x.shape, F32)
    for c in range(d_ff // ff_tile):
        cols = slice(c * ff_tile, (c + 1) * ff_tile)
        a = jnp.maximum(_dot(hb, w1_ref[:, cols]), 0.0)
        acc = acc + _dot((a * a).astype(BF16), w2_ref[cols, :])
    y = x + mod_ref[0, 5:6, :] * acc
    if final_norm:
        y = y * lax.rsqrt(jnp.mean(y * y, axis=-1, keepdims=True) + NORM_EPS) * fg_ref[...]
    o_ref[0] = y


def _post(x, mod, scans, gates, y_attn, w_out, norm_g, w1, w2, final_g, *, tm, final_norm):
    bsz, n, d = x.shape
    tm = min(tm, n)
    tok = lambda w: pl.BlockSpec((1, tm, w), lambda b, i: (b, i, 0))
    full = lambda a: pl.BlockSpec(a.shape, lambda b, i: (0,) * a.ndim, pipeline_mode=pl.Buffered(1))
    rf, rb, gf, gb = scans
    rg, gg = gates
    return pl.pallas_call(
        functools.partial(_post_kernel, ff_tile=1024, final_norm=final_norm),
        grid=(bsz, n // tm),
        in_specs=[tok(d), pl.BlockSpec((1, 6, d), lambda b, i: (b, 0, 0)),
                  tok(256), tok(256), tok(256), tok(256), tok(256), tok(256), tok(512),
                  full(w_out), full(norm_g), full(w1), full(w2), full(final_g)],
        out_specs=tok(d),
        out_shape=jax.ShapeDtypeStruct((bsz, n, d), F32),
        compiler_params=pltpu.CompilerParams(dimension_semantics=("parallel", "parallel"),
                                             vmem_limit_bytes=V7X_VMEM_LIMIT_BYTES),
        name="post_final" if final_norm else "post",
    )(x, mod, rf, rb, rg, gf, gb, gg, y_attn, w_out, norm_g, w1, w2, final_g)


def _prep_w_in(w):
    d = w.shape[0]
    ret = w[:, 0:768]
    gla = w[:, 768:1536]
    low = w[:, 1536:1568]
    aq = w[:, 1568:2080]
    ak = w[:, 2080:2208]
    av = w[:, 2208:2336]
    w_nat = jnp.concatenate([ret, gla, low, jnp.zeros((d, 128 - 2 * GLA_RANK), w.dtype)], axis=1)
    w_tr = jnp.concatenate([aq, ak, av], axis=1).T
    return w_nat.astype(BF16), w_tr.astype(BF16)


def _rope_tables(n_lat):
    t = jnp.arange(n_lat)
    row = (t // GRID_W).astype(F32)
    col = (t % GRID_W).astype(F32)

    def head_tables(head_dim):
        nf = head_dim // 4
        inv_freq = ROPE_BASE ** (-jnp.arange(nf, dtype=F32) / nf)
        ang_r = row[:, None] * inv_freq
        ang_c = col[:, None] * inv_freq
        cos = jnp.concatenate([jnp.cos(ang_r)] * 2 + [jnp.cos(ang_c)] * 2, axis=1)
        sin = jnp.concatenate([-jnp.sin(ang_r), jnp.sin(ang_r), -jnp.sin(ang_c), jnp.sin(ang_c)], axis=1)
        return cos, sin

    cos_r, sin_r = head_tables(RET_DK)
    cos_r = jnp.tile(cos_r, (1, RET_HEADS))
    sin_r = jnp.tile(sin_r, (1, RET_HEADS))
    cos_a, sin_a = head_tables(GQA_HEAD_DIM)
    return cos_r, sin_r, cos_a.T, sin_a.T


def kernel(x, c, ctx, c_ctx, mod_w, mod_b, attn_norm_g, mlp_norm_g, w_in, w_out, ret_decay_logit,
           gla_gate_w, gla_gate_b, qk_norm_g, mlp_w1, mlp_w2, final_norm_g):
    bsz, n_lat, d = x.shape
    n_ctx = ctx.shape[1]
    depth = mod_w.shape[0]
    scan_tile = 512
    assert n_lat % scan_tile == 0 and n_ctx % SCAN_CHUNK == 0 and bsz <= 7

    cvec = jnp.zeros((8, d), F32).at[:bsz].set(c).at[bsz].set(c_ctx)
    mods = _modulation(cvec, mod_w, mod_b)
    lat_tables = _rope_tables(n_lat)
    ctx_tables = tuple(jnp.zeros((n_ctx, 128), F32) for _ in range(2)) + \
        tuple(jnp.zeros((GQA_HEAD_DIM, n_ctx), F32) for _ in range(2))
    final_g = final_norm_g.reshape(1, d)

    for i in range(depth):
        need_ctx = i < depth - 1
        mod_lat = mods[i, :bsz].reshape(bsz, 6, d)
        mod_ctx = jnp.broadcast_to(mods[i, bsz].reshape(1, 6, d), (bsz, 6, d))
        w_nat, w_tr = _prep_w_in(w_in[i])
        gw = jnp.zeros((128, 256), F32)
        gw = gw.at[0:GLA_RANK, 0:128].set(gla_gate_w[i, 0]).at[GLA_RANK:2 * GLA_RANK, 128:256].set(gla_gate_w[i, 1])
        gw = gw.astype(BF16)
        gb = gla_gate_b[i].reshape(1, 256)
        q_gain = qk_norm_g[i, 0].reshape(GQA_HEAD_DIM, 1)
        k_gain = qk_norm_g[i, 1].reshape(GQA_HEAD_DIM, 1)
        attn_g = attn_norm_g[i].reshape(1, d)
        mlp_g = mlp_norm_g[i].reshape(1, d)
        decay = jnp.repeat(ret_decay_logit[i], RET_DK, axis=1)
        w_o = w_out[i].astype(BF16)
        w1 = mlp_w1[i].astype(BF16)
        w2 = mlp_w2[i].astype(BF16)

        pl_lat = _in_proj(x, mod_lat, attn_g, w_nat, w_tr, gw, gb, q_gain, k_gain, lat_tables, rope=True, tm=512)
        pl_ctx = _in_proj(ctx, mod_ctx, attn_g, w_nat, w_tr, gw, gb, q_gain, k_gain, ctx_tables, rope=False,
                          tm=n_ctx)
        ret = _bidir_scan(pl_lat[0:3], pl_ctx[0:3], None, None, decay, tile=scan_tile)
        gla = _bidir_scan(pl_lat[4:7], pl_ctx[4:7], pl_lat[8:10], pl_ctx[8:10], None, tile=scan_tile)
        ak = jnp.concatenate([pl_lat[11], pl_ctx[11]], axis=1)
        av = jnp.concatenate([pl_lat[12], pl_ctx[12]], axis=3)
        y_attn = _attention(pl_lat[10], ak, av, tq=256, tk=8320)

        x = _post(x, mod_lat, (ret[0], ret[1], gla[0], gla[1]), (pl_lat[3], pl_lat[7]), y_attn, w_o,
                  mlp_g, w1, w2, final_g, tm=512, final_norm=not need_ctx)
        if need_ctx:
            y_attn_c = _attention(pl_ctx[10], pl_ctx[11], pl_ctx[12], tq=256, tk=256)
            ctx = _post(ctx, mod_ctx, (ret[2], ret[3], gla[2], gla[3]), (pl_ctx[3], pl_ctx[7]), y_attn_c, w_o,
                        mlp_g, w1, w2, final_g, tm=n_ctx, final_norm=False)
    return x
```

```python
import functools

import jax
import jax.numpy as jnp
import numpy as np
from jax import lax
from jax.experimental import pallas as pl
from jax.experimental.pallas import tpu as pltpu

F32 = jnp.float32
BF16 = jnp.bfloat16

GRID_W = 64
RET_HEADS = 4
RET_DK = 32
RET_DV = 64
GLA_HEADS = 4
GLA_DK = 32
GLA_DV = 64
GLA_RANK = 16
GLA_TAU = 16.0
GQA_HEADS = 8
GQA_KV_HEADS = 2
GQA_GROUP = 4
GQA_HEAD_DIM = 64
SCAN_CHUNK = 64
ROPE_BASE = 10000.0
NORM_EPS = 1e-6

V7X_VMEM_LIMIT_BYTES = 56 * 1024 * 1024
V_ROWS = 80
ATTN_SUB_KEYS = 128
ATTN_COL_BLOCK = 256
ATTN_LOOKAHEAD = 8
LOG2_E = 1.4426950408889634

NAT_WIDTH = 1664
TR_WIDTH = 768


def _silu(x):
    return x * (1.0 / (1.0 + jnp.exp(-x)))


def _log_sigmoid(z):
    return jnp.minimum(z, 0.0) - jnp.log(1.0 + jnp.exp(-jnp.abs(z)))


def _dot(a, b):
    return jnp.dot(a, b, preferred_element_type=F32)


def _dot_nt(a, b):
    return lax.dot_general(a, b, (((1,), (1,)), ((), ())), preferred_element_type=F32)


def _dot_tn(a, b):
    return lax.dot_general(a, b, (((0,), (0,)), ((), ())), preferred_element_type=F32)


def _mod_kernel(c_ref, w_ref, b_ref, o_ref):
    s = _silu(c_ref[...]).astype(BF16)
    o_ref[0] = _dot(s, w_ref[0].astype(BF16)) + b_ref[0]


def _modulation(cvec, mod_w, mod_b):
    depth, d, n = mod_w.shape
    tn = 1536
    return pl.pallas_call(
        _mod_kernel,
        grid=(depth, n // tn),
        in_specs=[pl.BlockSpec((8, d), lambda l, j: (0, 0)),
                  pl.BlockSpec((1, d, tn), lambda l, j: (l, 0, j)),
                  pl.BlockSpec((1, 1, tn), lambda l, j: (l, 0, j))],
        out_specs=pl.BlockSpec((1, 8, tn), lambda l, j: (l, 0, j)),
        out_shape=jax.ShapeDtypeStruct((depth, 8, n), F32),
        compiler_params=pltpu.CompilerParams(dimension_semantics=("parallel", "parallel"),
                                             vmem_limit_bytes=V7X_VMEM_LIMIT_BYTES),
        name="modulation",
    )(cvec, mod_w, mod_b.reshape(depth, 1, n))


def _rope_nat(x, cos, sin_signed, nf):
    lane = lax.broadcasted_iota(jnp.int32, x.shape, 1)
    first = (lane % (2 * nf)) < nf
    partner = jnp.where(first, pltpu.roll(x, 128 - nf, 1), pltpu.roll(x, nf, 1))
    return x * cos + partner * sin_signed


def _head_norm_rope_t(xt, g_col, cos_t, sin_t, rope):
    h = xt.shape[0] // GQA_HEAD_DIM
    x3 = xt.reshape(h, GQA_HEAD_DIM, xt.shape[1])
    ms = jnp.sum(x3 * x3, axis=1, keepdims=True) * (1.0 / GQA_HEAD_DIM)
    x3 = x3 * lax.rsqrt(ms + NORM_EPS) * g_col[None]
    if rope:
        swapped = jnp.concatenate([x3[:, 16:32], x3[:, 0:16], x3[:, 48:64], x3[:, 32:48]], axis=1)
        x3 = x3 * cos_t[None] + swapped * sin_t[None]
    return x3


def _in_proj_kernel(x_ref, mod_ref, g_ref, wn_ref, wt_ref, gw_ref, gb_ref, qg_ref, kg_ref,
                    cr_ref, sr_ref, ct_ref, st_ref,
                    rq_ref, rk_ref, rv_ref, rg_ref, gq_ref, gk_ref, gv_ref, gg_ref, gf_ref, gbw_ref,
                    aq_ref, ak_ref, av_ref, *, rope):
    x = x_ref[0]
    ms = jnp.mean(x * x, axis=-1, keepdims=True)
    h = x * lax.rsqrt(ms + NORM_EPS) * g_ref[...]
    h = h * (1.0 + mod_ref[0, 1:2, :]) + mod_ref[0, 0:1, :]
    hb = h.astype(BF16)
    p = _dot(hb, wn_ref[...])
    pt = _dot_nt(wt_ref[...], hb)

    rq = p[:, 0:128]
    rk = p[:, 128:256]
    gq = p[:, 768:896]
    if rope:
        cr = cr_ref[...]
        sr = sr_ref[...]
        rq = _rope_nat(rq, cr, sr, RET_DK // 4)
        rk = _rope_nat(rk, cr, sr, RET_DK // 4)
    rq_ref[0] = rq
    rk_ref[0] = rk * (RET_DK ** -0.5)
    rv_ref[0] = p[:, 256:512].astype(BF16)
    rg_ref[0] = _silu(p[:, 512:768])
    gq_ref[0] = gq * (GLA_DK ** -0.5)
    gk_ref[0] = p[:, 896:1024]
    gv_ref[0] = p[:, 1024:1280].astype(BF16)
    gg_ref[0] = _silu(p[:, 1280:1536])
    z = _dot(p[:, 1536:1664].astype(BF16), gw_ref[...]) + gb_ref[...]
    gates = _log_sigmoid(z) * (1.0 / GLA_TAU)
    gf_ref[0] = gates[:, 0:128]
    gbw_ref[0] = gates[:, 128:256]

    ct = ct_ref[...]
    st = st_ref[...]
    q3 = _head_norm_rope_t(pt[0:512], qg_ref[...], ct, st, rope)
    aq_ref[0] = (q3 * (GQA_HEAD_DIM ** -0.5 * LOG2_E)).astype(BF16)
    k3 = _head_norm_rope_t(pt[512:640], kg_ref[...], ct, st, rope)
    ak_ref[0] = k3.reshape(GQA_KV_HEADS * GQA_HEAD_DIM, k3.shape[2]).T.astype(BF16)
    av_ref[0] = pt[640:768].reshape(GQA_KV_HEADS, GQA_HEAD_DIM, pt.shape[1]).astype(BF16)


def _in_proj(x, mod, norm_g, w_nat, w_tr, gate_w_bd, gate_b_cat, q_gain, k_gain, tables, *, rope, tm):
    bsz, n, d = x.shape
    tm = min(tm, n)
    cos_r, sin_r, cos_t, sin_t = tables
    tok = lambda w: pl.BlockSpec((1, tm, w), lambda b, i: (b, i, 0))
    full = lambda a: pl.BlockSpec(a.shape, lambda b, i: (0,) * a.ndim)
    f32_out = lambda w: jax.ShapeDtypeStruct((bsz, n, w), F32)
    outs = pl.pallas_call(
        functools.partial(_in_proj_kernel, rope=rope),
        grid=(bsz, n // tm),
        in_specs=[tok(d),
                  pl.BlockSpec((1, 6, d), lambda b, i: (b, 0, 0)),
                  full(norm_g), full(w_nat), full(w_tr), full(gate_w_bd), full(gate_b_cat),
                  full(q_gain), full(k_gain),
                  pl.BlockSpec((tm, 128), lambda b, i: (i, 0)),
                  pl.BlockSpec((tm, 128), lambda b, i: (i, 0)),
                  pl.BlockSpec((GQA_HEAD_DIM, tm), lambda b, i: (0, i)),
                  pl.BlockSpec((GQA_HEAD_DIM, tm), lambda b, i: (0, i))],
        out_specs=[tok(128), tok(128), tok(256), tok(256), tok(128), tok(128), tok(256), tok(256),
                   tok(128), tok(128),
                   pl.BlockSpec((1, GQA_HEADS, GQA_HEAD_DIM, tm), lambda b, i: (b, 0, 0, i)),
                   tok(128),
                   pl.BlockSpec((1, GQA_KV_HEADS, GQA_HEAD_DIM, tm), lambda b, i: (b, 0, 0, i))],
        out_shape=[f32_out(128), f32_out(128), jax.ShapeDtypeStruct((bsz, n, 256), BF16), f32_out(256),
                   f32_out(128), f32_out(128), jax.ShapeDtypeStruct((bsz, n, 256), BF16), f32_out(256),
                   f32_out(128), f32_out(128),
                   jax.ShapeDtypeStruct((bsz, GQA_HEADS, GQA_HEAD_DIM, n), BF16),
                   jax.ShapeDtypeStruct((bsz, n, 128), BF16),
                   jax.ShapeDtypeStruct((bsz, GQA_KV_HEADS, GQA_HEAD_DIM, n), BF16)],
        compiler_params=pltpu.CompilerParams(dimension_semantics=("parallel", "parallel"),
                                             vmem_limit_bytes=V7X_VMEM_LIMIT_BYTES),
        name="in_proj_rope" if rope else "in_proj_ctx",
    )(x, mod, norm_g, w_nat, w_tr, gate_w_bd, gate_b_cat, q_gain, k_gain, cos_r, sin_r, cos_t, sin_t)
    return outs


def _cumsum_rows(g, reverse):
    c = g.shape[0]
    row = lax.broadcasted_iota(jnp.int32, g.shape, 0)
    b = g
    s = 1
    while s < c:
        if reverse:
            b = b + jnp.where(row < c - s, pltpu.roll(b, c - s, 0), 0.0)
        else:
            b = b + jnp.where(row >= s, pltpu.roll(b, s, 0), 0.0)
        s *= 2
    return b


def _scan_chunk_local(q, k, v, b, b_end, reverse):
    c = q.shape[0]
    qd = (q * jnp.exp(b)).astype(BF16)
    kd = (k * jnp.exp(-b)).astype(BF16)
    ku = (k * jnp.exp(b_end - b)).astype(BF16)
    vb = v.astype(BF16)

    lane_k = lax.broadcasted_iota(jnp.int32, (c, 128), 1) // RET_DK
    lane_v = lax.broadcasted_iota(jnp.int32, (c, 256), 1) // RET_DV
    zk = jnp.zeros_like(kd)
    zv = jnp.zeros_like(vb)
    kd_stack = jnp.concatenate([jnp.where(lane_k == h, kd, zk) for h in range(RET_HEADS)], axis=0)
    v_stack = jnp.concatenate([jnp.where(lane_v == h, vb, zv) for h in range(RET_HEADS)], axis=0)

    a = _dot_nt(qd, kd_stack)
    u = _dot_tn(vb, ku)
    r = lax.broadcasted_iota(jnp.int32, u.shape, 0) // RET_DV
    l = lax.broadcasted_iota(jnp.int32, u.shape, 1) // RET_DK
    return qd, a, v_stack, jnp.where(r == l, u, 0.0), jnp.exp(b_end)


def _scan_chunk_intra(a, v_stack, reverse):
    c = a.shape[0]
    i = lax.broadcasted_iota(jnp.int32, a.shape, 0)
    j = lax.broadcasted_iota(jnp.int32, a.shape, 1) % c
    keep = (j > i) if reverse else (j <= i)
    return _dot(jnp.where(keep, a, 0.0).astype(BF16), v_stack)


def _scan_tile(fwd_refs, bwd_refs, out_refs, st_ref, lg_ref, *, gated, tile):
    c = SCAN_CHUNK
    n_chunks = tile // c
    order = {0: list(range(n_chunks)), 1: list(range(n_chunks - 1, -1, -1))}
    local = {}
    for direction in range(2):
        reverse = direction == 1
        q_ref, k_ref, v_ref, g_ref = bwd_refs if reverse else fwd_refs
        for ci in order[direction]:
            rows = pl.ds(ci * c, c)
            if gated:
                b = _cumsum_rows(g_ref[0, rows, :], reverse)
                b_end = b[0:1, :] if reverse else b[c - 1:c, :]
            else:
                lg = _log_sigmoid(lg_ref[direction:direction + 1, :])
                row = lax.broadcasted_iota(jnp.int32, (c, 128), 0)
                steps = (c - row) if reverse else (row + 1)
                b = steps.astype(F32) * lg
                b_end = float(c) * lg
            local[direction, ci] = _scan_chunk_local(q_ref[0, rows, :], k_ref[0, rows, :], v_ref[0, rows, :],
                                                     b, b_end, reverse)

    for direction in range(2):
        for ci in order[direction]:
            qd, a, v_stack, u, decay = local[direction, ci]
            out_refs[direction][0, pl.ds(ci * c, c), :] = _scan_chunk_intra(a, v_stack, direction == 1)
            local[direction, ci] = (qd, u, decay)

    start_state = {}
    for direction in range(2):
        st = st_ref[direction]
        for ci in order[direction]:
            _, u, decay = local[direction, ci]
            start_state[direction, ci] = st.astype(BF16)
            st = st * decay + u
        st_ref[direction] = st

    for direction in range(2):
        o_ref = out_refs[direction]
        for ci in order[direction]:
            rows = pl.ds(ci * c, c)
            o_ref[0, rows, :] = o_ref[0, rows, :] + _dot_nt(local[direction, ci][0], start_state[direction, ci])


def _scan_kernel(*refs, gated, tile, ctx_tile):
    if gated:
        (qf, kf, vf, gf, qb, kb, vb, gb, qc, kc, vc, gfc, gbc, of, ob, ocf, ocb, st_ref) = refs
        lg_ref = None
    else:
        (qf, kf, vf, qb, kb, vb, qc, kc, vc, lg_ref, of, ob, ocf, ocb, st_ref) = refs
        gf = gb = gfc = gbc = None
    step = pl.program_id(1)
    tile_fn = functools.partial(_scan_tile, st_ref=st_ref, lg_ref=lg_ref, gated=gated)

    @pl.when(step == 0)
    def _():
        st_ref[...] = jnp.zeros_like(st_ref)
        tile_fn((qc, kc, vc, gfc), (qc, kc, vc, gbc), (ocf, ocb), tile=ctx_tile)

    @pl.when(step > 0)
    def _():
        tile_fn((qf, kf, vf, gf), (qb, kb, vb, gb), (of, ob), tile=tile)


def _bidir_scan(lat, ctx, gates_lat, gates_ctx, decay_logit, *, tile):
    q, k, v = lat
    bsz, n_lat, _ = q.shape
    tile = min(tile, n_lat)
    n_lat_tiles = n_lat // tile
    ctx_tile = ctx[0].shape[1]
    fwd = lambda b, s: (b, jnp.maximum(s - 1, 0), 0)
    bwd = lambda b, s: (b, n_lat_tiles - 1 - jnp.maximum(s - 1, 0), 0)
    one = lambda b, s: (b, 0, 0)
    spec = lambda w, im: pl.BlockSpec((1, ctx_tile if im is one else tile, w), im)
    qkv = lambda im: [spec(128, im), spec(128, im), spec(256, im)]
    gated = gates_lat is not None
    if gated:
        args = (q, k, v, gates_lat[0], q, k, v, gates_lat[1], *ctx, *gates_ctx)
        in_specs = (qkv(fwd) + [spec(128, fwd)] + qkv(bwd) + [spec(128, bwd)] + qkv(one)
                    + [spec(128, one), spec(128, one)])
    else:
        args = (q, k, v, q, k, v, *ctx, decay_logit)
        in_specs = qkv(fwd) + qkv(bwd) + qkv(one) + [pl.BlockSpec(decay_logit.shape, lambda b, s: (0, 0))]
    return pl.pallas_call(
        functools.partial(_scan_kernel, gated=gated, tile=tile, ctx_tile=ctx_tile),
        grid=(bsz, n_lat_tiles + 1),
        in_specs=in_specs,
        out_specs=[spec(256, fwd), spec(256, bwd), spec(256, one), spec(256, one)],
        out_shape=([jax.ShapeDtypeStruct((bsz, n_lat, 256), F32)] * 2
                   + [jax.ShapeDtypeStruct((bsz, ctx_tile, 256), F32)] * 2),
        scratch_shapes=[pltpu.VMEM((2, 256, 128), F32)],
        compiler_params=pltpu.CompilerParams(dimension_semantics=("parallel", "arbitrary"),
                                             vmem_limit_bytes=V7X_VMEM_LIMIT_BYTES),
        name="scan_gla" if gated else "scan_ret",
    )(*args)


def _attn_kernel(q_ref, k_ref, v_ref, o_ref, acc_ref, qpad_ref, *, n_kv_tiles):
    kv = pl.program_id(1)
    tq = q_ref.shape[3]
    tk = k_ref.shape[2]
    width = GQA_GROUP * tq
    qcat = jnp.concatenate([q_ref[0, g] for g in range(GQA_GROUP)], axis=1)
    zero = jnp.zeros_like(qcat)
    qpad_ref[...] = jnp.where(kv == 0, jnp.concatenate([qcat, zero], axis=0),
                              jnp.concatenate([zero, qcat], axis=0))
    acc_ref[...] = jnp.zeros_like(acc_ref)

    def body(j, m_all):
        n_cb = width // ATTN_COL_BLOCK
        m_parts = [m_all[:, cb * ATTN_COL_BLOCK:(cb + 1) * ATTN_COL_BLOCK] for cb in range(n_cb)]
        n_pairs = tk // (2 * ATTN_SUB_KEYS)
        units = [(2 * pr + h, cb) for pr in range(n_pairs) for cb in range(n_cb) for h in range(2)]
        rows = lambda sub, n=1: slice(sub * ATTN_SUB_KEYS, (sub + n) * ATTN_SUB_KEYS)
        cols = lambda cb: slice(cb * ATTN_COL_BLOCK, (cb + 1) * ATTN_COL_BLOCK)

        def scores(u):
            sub, cb = units[u]
            return _dot(k_ref[0, j, rows(sub), :], qpad_ref[:, cols(cb)])

        pending = [scores(u) for u in range(min(ATTN_LOOKAHEAD, len(units)))]
        for u in range(0, len(units), 2):
            sub, cb = units[u]
            for ahead in (u + ATTN_LOOKAHEAD, u + ATTN_LOOKAHEAD + 1):
                if ahead < len(units):
                    pending.append(scores(ahead))
            s_a = pending.pop(0).astype(BF16)
            s_b = pending.pop(0).astype(BF16)
            m_a = jnp.maximum(m_parts[cb], jnp.max(s_a, axis=0, keepdims=True))
            p_a = jnp.exp2(s_a - m_a)
            m_b = jnp.maximum(m_a, jnp.max(s_b, axis=0, keepdims=True))
            p_b = jnp.exp2(s_b - m_b)
            p = jnp.concatenate([p_a * jnp.exp2(m_a - m_b), p_b], axis=0)
            alpha = jnp.exp2(m_parts[cb].astype(F32) - m_b.astype(F32))
            acc_ref[:, cols(cb)] = acc_ref[:, cols(cb)] * alpha + _dot(v_ref[0, 0, j, :, rows(sub, 2)], p)
            m_parts[cb] = m_b
        return jnp.concatenate(m_parts, axis=1)

    lax.fori_loop(0, n_kv_tiles, body, jnp.full((1, width), -jnp.inf, BF16))

    acc = acc_ref[...]
    o = acc[0:GQA_HEAD_DIM] * (1.0 / acc[GQA_HEAD_DIM:GQA_HEAD_DIM + 1])
    o_ref[0] = jnp.concatenate([o[:, g * tq:(g + 1) * tq].T for g in range(GQA_GROUP)], axis=1).astype(o_ref.dtype)


def _attention(q_t, k_nat, v_t, *, tq, tk):
    bsz, _, _, nq = q_t.shape
    ns = k_nat.shape[1]
    tq = min(tq, nq)
    tk = min(tk, ns)
    n_kv_tiles = ns // tk
    k_tiles = k_nat.reshape(bsz, n_kv_tiles, tk, 128)
    ones = jnp.ones((bsz, GQA_KV_HEADS, V_ROWS - GQA_HEAD_DIM, ns), BF16)
    v_tiles = jnp.concatenate([v_t, ones], axis=2).reshape(bsz, GQA_KV_HEADS, V_ROWS, n_kv_tiles, tk)
    v_tiles = v_tiles.transpose(0, 1, 3, 2, 4)
    return pl.pallas_call(
        functools.partial(_attn_kernel, n_kv_tiles=n_kv_tiles),
        grid=(bsz, GQA_KV_HEADS, nq // tq),
        in_specs=[pl.BlockSpec((1, GQA_GROUP, GQA_HEAD_DIM, tq), lambda b, h, i: (b, h, 0, i)),
                  pl.BlockSpec((1, n_kv_tiles, tk, 128), lambda b, h, i: (b, 0, 0, 0)),
                  pl.BlockSpec((1, 1, n_kv_tiles, V_ROWS, tk), lambda b, h, i: (b, h, 0, 0, 0))],
        out_specs=pl.BlockSpec((1, tq, GQA_GROUP * GQA_HEAD_DIM), lambda b, h, i: (b, i, h)),
        out_shape=jax.ShapeDtypeStruct((bsz, nq, GQA_HEADS * GQA_HEAD_DIM), BF16),
        scratch_shapes=[pltpu.VMEM((V_ROWS, GQA_GROUP * tq), F32),
                        pltpu.VMEM((2 * GQA_HEAD_DIM, GQA_GROUP * tq), BF16)],
        compiler_params=pltpu.CompilerParams(dimension_semantics=("parallel", "parallel", "parallel"),
                                             vmem_limit_bytes=V7X_VMEM_LIMIT_BYTES),
        name="gqa_attention",
    )(q_t, k_tiles, v_tiles)


def _group_mean(x, gmat):
    hi = x.astype(BF16)
    lo = (x - hi.astype(F32)).astype(BF16)
    return (_dot(hi, gmat) + _dot(lo, gmat)) * (1.0 / RET_DV)


def _post_kernel(x_ref, mod_ref, rf_ref, rb_ref, rg_ref, gf_ref, gb_ref, gg_ref, ya_ref, wo_ref,
                 g_ref, w1_ref, w2_ref, fg_ref, o_ref, *, ff_tile, final_norm):
    r = lax.broadcasted_iota(jnp.int32, (256, 256), 0) // RET_DV
    c = lax.broadcasted_iota(jnp.int32, (256, 256), 1) // RET_DV
    gmat = jnp.where(r == c, 1.0, 0.0).astype(BF16)
    ro = rf_ref[0] + rb_ref[0]
    rc = ro - _group_mean(ro, gmat)
    y_ret = rg_ref[0] * (rc * lax.rsqrt(_group_mean(rc * rc, gmat) + NORM_EPS))
    go = gf_ref[0] + gb_ref[0]
    y_gla = gg_ref[0] * (go * lax.rsqrt(_group_mean(go * go, gmat) + NORM_EPS))
    y = jnp.concatenate([y_ret.astype(BF16), y_gla.astype(BF16), ya_ref[0]], axis=1)
    x = x_ref[0] + mod_ref[0, 2:3, :] * _dot(y, wo_ref[...])

    ms = jnp.mean(x * x, axis=-1, keepdims=True)
    h = x * lax.rsqrt(ms + NORM_EPS) * g_ref[...]
    hb = (h * (1.0 + mod_ref[0, 4:5, :]) + mod_ref[0, 3:4, :]).astype(BF16)
    d_ff = w1_ref.shape[1]
    acc = jnp.zeros(x.shape, F32)
    for c in range(d_ff // ff_tile):
        cols = slice(c * ff_tile, (c + 1) * ff_tile)
        a = jnp.maximum(_dot(hb, w1_ref[:, cols]), 0.0)
        acc = acc + _dot((a * a).astype(BF16), w2_ref[cols, :])
    y = x + mod_ref[0, 5:6, :] * acc
    if final_norm:
        y = y * lax.rsqrt(jnp.mean(y * y, axis=-1, keepdims=True) + NORM_EPS) * fg_ref[...]
    o_ref[0] = y


def _post(x, mod, scans, gates, y_attn, w_out, norm_g, w1, w2, final_g, *, tm, final_norm):
    bsz, n, d = x.shape
    tm = min(tm, n)
    tok = lambda w: pl.BlockSpec((1, tm, w), lambda b, i: (b, i, 0))
    full = lambda a: pl.BlockSpec(a.shape, lambda b, i: (0,) * a.ndim, pipeline_mode=pl.Buffered(1))
    rf, rb, gf, gb = scans
    rg, gg = gates
    return pl.pallas_call(
        functools.partial(_post_kernel, ff_tile=1024, final_norm=final_norm),
        grid=(bsz, n // tm),
        in_specs=[tok(d), pl.BlockSpec((1, 6, d), lambda b, i: (b, 0, 0)),
                  tok(256), tok(256), tok(256), tok(256), tok(256), tok(256), tok(512),
                  full(w_out), full(norm_g), full(w1), full(w2), full(final_g)],
        out_specs=tok(d),
        out_shape=jax.ShapeDtypeStruct((bsz, n, d), F32),
        compiler_params=pltpu.CompilerParams(dimension_semantics=("parallel", "parallel"),
                                             vmem_limit_bytes=V7X_VMEM_LIMIT_BYTES),
        name="post_final" if final_norm else "post",
    )(x, mod, rf, rb, rg, gf, gb, gg, y_attn, w_out, norm_g, w1, w2, final_g)


def _prep_w_in(w):
    d = w.shape[0]
    ret = w[:, 0:768]
    gla = w[:, 768:1536]
    low = w[:, 1536:1568]
    aq = w[:, 1568:2080]
    ak = w[:, 2080:2208]
    av = w[:, 2208:2336]
    w_nat = jnp.concatenate([ret, gla, low, jnp.zeros((d, 128 - 2 * GLA_RANK), w.dtype)], axis=1)
    w_tr = jnp.concatenate([aq, ak, av], axis=1).T
    return w_nat.astype(BF16), w_tr.astype(BF16)


def _rope_tables(n_lat):
    t = jnp.arange(n_lat)
    row = (t // GRID_W).astype(F32)
    col = (t % GRID_W).astype(F32)

    def head_tables(head_dim):
        nf = head_dim // 4
        inv_freq = ROPE_BASE ** (-jnp.arange(nf, dtype=F32) / nf)
        ang_r = row[:, None] * inv_freq
        ang_c = col[:, None] * inv_freq
        cos = jnp.concatenate([jnp.cos(ang_r)] * 2 + [jnp.cos(ang_c)] * 2, axis=1)
        sin = jnp.concatenate([-jnp.sin(ang_r), jnp.sin(ang_r), -jnp.sin(ang_c), jnp.sin(ang_c)], axis=1)
        return cos, sin

    cos_r, sin_r = head_tables(RET_DK)
    cos_r = jnp.tile(cos_r, (1, RET_HEADS))
    sin_r = jnp.tile(sin_r, (1, RET_HEADS))
    cos_a, sin_a = head_tables(GQA_HEAD_DIM)
    return cos_r, sin_r, cos_a.T, sin_a.T


def kernel(x, c, ctx, c_ctx, mod_w, mod_b, attn_norm_g, mlp_norm_g, w_in, w_out, ret_decay_logit,
           gla_gate_w, gla_gate_b, qk_norm_g, mlp_w1, mlp_w2, final_norm_g):
    bsz, n_lat, d = x.shape
    n_ctx = ctx.shape[1]
    depth = mod_w.shape[0]
    scan_tile = 512
    assert n_lat % scan_tile == 0 and n_ctx % SCAN_CHUNK == 0 and bsz <= 7

    cvec = jnp.zeros((8, d), F32).at[:bsz].set(c).at[bsz].set(c_ctx)
    mods = _modulation(cvec, mod_w, mod_b)
    lat_tables = _rope_tables(n_lat)
    ctx_tables = tuple(jnp.zeros((n_ctx, 128), F32) for _ in range(2)) + \
        tuple(jnp.zeros((GQA_HEAD_DIM, n_ctx), F32) for _ in range(2))
    final_g = final_norm_g.reshape(1, d)

    for i in range(depth):
        need_ctx = i < depth - 1
        mod_lat = mods[i, :bsz].reshape(bsz, 6, d)
        mod_ctx = jnp.broadcast_to(mods[i, bsz].reshape(1, 6, d), (bsz, 6, d))
        w_nat, w_tr = _prep_w_in(w_in[i])
        gw = jnp.zeros((128, 256), F32)
        gw = gw.at[0:GLA_RANK, 0:128].set(gla_gate_w[i, 0]).at[GLA_RANK:2 * GLA_RANK, 128:256].set(gla_gate_w[i, 1])
        gw = gw.astype(BF16)
        gb = gla_gate_b[i].reshape(1, 256)
        q_gain = qk_norm_g[i, 0].reshape(GQA_HEAD_DIM, 1)
        k_gain = qk_norm_g[i, 1].reshape(GQA_HEAD_DIM, 1)
        attn_g = attn_norm_g[i].reshape(1, d)
        mlp_g = mlp_norm_g[i].reshape(1, d)
        decay = jnp.repeat(ret_decay_logit[i], RET_DK, axis=1)
        w_o = w_out[i].astype(BF16)
        w1 = mlp_w1[i].astype(BF16)
        w2 = mlp_w2[i].astype(BF16)

        pl_lat = _in_proj(x, mod_lat, attn_g, w_nat, w_tr, gw, gb, q_gain, k_gain, lat_tables, rope=True, tm=512)
        pl_ctx = _in_proj(ctx, mod_ctx, attn_g, w_nat, w_tr, gw, gb, q_gain, k_gain, ctx_tables, rope=False,
                          tm=n_ctx)
        ret = _bidir_scan(pl_lat[0:3], pl_ctx[0:3], None, None, decay, tile=scan_tile)
        gla = _bidir_scan(pl_lat[4:7], pl_ctx[4:7], pl_lat[8:10], pl_ctx[8:10], None, tile=scan_tile)
        ak = jnp.concatenate([pl_lat[11], pl_ctx[11]], axis=1)
        av = jnp.concatenate([pl_lat[12], pl_ctx[12]], axis=3)
        y_attn = _attention(pl_lat[10], ak, av, tq=512, tk=8320)

        x = _post(x, mod_lat, (ret[0], ret[1], gla[0], gla[1]), (pl_lat[3], pl_lat[7]), y_attn, w_o,
                  mlp_g, w1, w2, final_g, tm=512, final_norm=not need_ctx)
        if need_ctx:
            y_attn_c = _attention(pl_ctx[10], pl_ctx[11], pl_ctx[12], tq=256, tk=256)
            ctx = _post(ctx, mod_ctx, (ret[2], ret[3], gla[2], gla[3]), (pl_ctx[3], pl_ctx[7]), y_attn_c, w_o,
                        mlp_g, w1, w2, final_g, tm=n_ctx, final_norm=False)
    return x
```

```python
import functools

import jax
import jax.numpy as jnp
import numpy as np
from jax import lax
from jax.experimental import pallas as pl
from jax.experimental.pallas import tpu as pltpu

F32 = jnp.float32
BF16 = jnp.bfloat16

GRID_W = 64
RET_HEADS = 4
RET_DK = 32
RET_DV = 64
GLA_HEADS = 4
GLA_DK = 32
GLA_DV = 64
GLA_RANK = 16
GLA_TAU = 16.0
GQA_HEADS = 8
GQA_KV_HEADS = 2
GQA_GROUP = 4
GQA_HEAD_DIM = 64
SCAN_CHUNK = 64
ROPE_BASE = 10000.0
NORM_EPS = 1e-6

V7X_VMEM_LIMIT_BYTES = 56 * 1024 * 1024
V_ROWS = 80
POST_FF_TILE = 1024
MOD_COL_TILE = 1536
IN_PROJ_SUB = 1024
ATTN_SUB_KEYS = 256
ATTN_COL_BLOCK = 256
ATTN_LOOKAHEAD = 5
LOG2_E = 1.4426950408889634

NAT_WIDTH = 1664
TR_WIDTH = 768


def _silu(x):
    return x * (1.0 / (1.0 + jnp.exp(-x)))


def _log_sigmoid(z):
    return jnp.minimum(z, 0.0) - jnp.log(1.0 + jnp.exp(-jnp.abs(z)))


def _dot(a, b):
    return jnp.dot(a, b, preferred_element_type=F32)


def _dot_nt(a, b):
    return lax.dot_general(a, b, (((1,), (1,)), ((), ())), preferred_element_type=F32)


def _dot_tn(a, b):
    return lax.dot_general(a, b, (((0,), (0,)), ((), ())), preferred_element_type=F32)


def _mod_kernel(c_ref, w_ref, b_ref, o_ref):
    s = _silu(c_ref[...]).astype(BF16)
    o_ref[0] = _dot(s, w_ref[0].astype(BF16)) + b_ref[0]


def _modulation(cvec, mod_w, mod_b):
    depth, d, n = mod_w.shape
    tn = MOD_COL_TILE
    assert n % tn == 0
    return pl.pallas_call(
        _mod_kernel,
        grid=(depth, n // tn),
        in_specs=[pl.BlockSpec((8, d), lambda l, j: (0, 0)),
                  pl.BlockSpec((1, d, tn), lambda l, j: (l, 0, j)),
                  pl.BlockSpec((1, 1, tn), lambda l, j: (l, 0, j))],
        out_specs=pl.BlockSpec((1, 8, tn), lambda l, j: (l, 0, j)),
        out_shape=jax.ShapeDtypeStruct((depth, 8, n), F32),
        compiler_params=pltpu.CompilerParams(dimension_semantics=("parallel", "parallel"),
                                             vmem_limit_bytes=V7X_VMEM_LIMIT_BYTES),
        name="modulation",
    )(cvec, mod_w, mod_b.reshape(depth, 1, n))


def _rope_nat(x, cos, sin_signed, nf):
    lane = lax.broadcasted_iota(jnp.int32, x.shape, 1)
    first = (lane % (2 * nf)) < nf
    partner = jnp.where(first, pltpu.roll(x, 128 - nf, 1), pltpu.roll(x, nf, 1))
    return x * cos + partner * sin_signed


def _head_norm_rope_t(xt, g_col, cos_t, sin_t, rope):
    h = xt.shape[0] // GQA_HEAD_DIM
    x3 = xt.reshape(h, GQA_HEAD_DIM, xt.shape[1])
    ms = jnp.sum(x3 * x3, axis=1, keepdims=True) * (1.0 / GQA_HEAD_DIM)
    x3 = x3 * lax.rsqrt(ms + NORM_EPS) * g_col[None]
    if rope:
        swapped = jnp.concatenate([x3[:, 16:32], x3[:, 0:16], x3[:, 48:64], x3[:, 32:48]], axis=1)
        x3 = x3 * cos_t[None] + swapped * sin_t[None]
    return x3


def _in_proj_kernel(x_ref, mod_ref, g_ref, wn_ref, wt_ref, gw_ref, gb_ref, qg_ref, kg_ref,
                    cr_ref, sr_ref, ct_ref, st_ref,
                    rq_ref, rk_ref, rv_ref, rg_ref, gq_ref, gk_ref, gv_ref, gg_ref, gf_ref, gbw_ref,
                    aq_ref, ak_ref, av_ref, *, rope, sub):
    n_sub = x_ref.shape[1] // sub
    rows_of = lambda r: slice(r * sub, (r + 1) * sub)

    def project(r):
        x = x_ref[0, rows_of(r), :]
        ms = jnp.mean(x * x, axis=-1, keepdims=True)
        h = x * lax.rsqrt(ms + NORM_EPS) * g_ref[...]
        h = h * (1.0 + mod_ref[0, 1:2, :]) + mod_ref[0, 0:1, :]
        hb = h.astype(BF16)
        return _dot(hb, wn_ref[...]), _dot_nt(wt_ref[...], hb)

    def finish(r, p, pt):
        rows = rows_of(r)
        rq = p[:, 0:128]
        rk = p[:, 128:256]
        gq = p[:, 768:896]
        if rope:
            cr = cr_ref[rows, :]
            sr = sr_ref[rows, :]
            rq = _rope_nat(rq, cr, sr, RET_DK // 4)
            rk = _rope_nat(rk, cr, sr, RET_DK // 4)
        rq_ref[0, rows, :] = rq
        rk_ref[0, rows, :] = rk * (RET_DK ** -0.5)
        rv_ref[0, rows, :] = p[:, 256:512].astype(BF16)
        rg_ref[0, rows, :] = _silu(p[:, 512:768])
        gq_ref[0, rows, :] = gq * (GLA_DK ** -0.5)
        gk_ref[0, rows, :] = p[:, 896:1024]
        gv_ref[0, rows, :] = p[:, 1024:1280].astype(BF16)
        gg_ref[0, rows, :] = _silu(p[:, 1280:1536])
        z = _dot(p[:, 1536:1664].astype(BF16), gw_ref[...]) + gb_ref[...]
        gates = _log_sigmoid(z) * (1.0 / GLA_TAU)
        gf_ref[0, rows, :] = gates[:, 0:128]
        gbw_ref[0, rows, :] = gates[:, 128:256]

        ct = ct_ref[:, rows]
        st = st_ref[:, rows]
        q3 = _head_norm_rope_t(pt[0:512], qg_ref[...], ct, st, rope)
        aq_ref[0, :, :, rows] = (q3 * (GQA_HEAD_DIM ** -0.5 * LOG2_E)).astype(BF16)
        k3 = _head_norm_rope_t(pt[512:640], kg_ref[...], ct, st, rope)
        ak_ref[0, rows, :] = k3.reshape(GQA_KV_HEADS * GQA_HEAD_DIM, sub).T.astype(BF16)
        av_ref[0, :, :, rows] = pt[640:768].reshape(GQA_KV_HEADS, GQA_HEAD_DIM, sub).astype(BF16)

    cur = project(0)
    for r in range(n_sub):
        nxt = project(r + 1) if r + 1 < n_sub else None
        finish(r, *cur)
        cur = nxt


def _in_proj(x, mod, norm_g, w_nat, w_tr, gate_w_bd, gate_b_cat, q_gain, k_gain, tables, *, rope, tm):
    bsz, n, d = x.shape
    tm = min(tm, n)
    assert n % tm == 0 and tm % min(tm, IN_PROJ_SUB) == 0
    cos_r, sin_r, cos_t, sin_t = tables
    tok = lambda w: pl.BlockSpec((1, tm, w), lambda b, i: (b, i, 0))
    full = lambda a: pl.BlockSpec(a.shape, lambda b, i: (0,) * a.ndim)
    f32_out = lambda w: jax.ShapeDtypeStruct((bsz, n, w), F32)
    outs = pl.pallas_call(
        functools.partial(_in_proj_kernel, rope=rope, sub=min(tm, IN_PROJ_SUB)),
        grid=(bsz, n // tm),
        in_specs=[tok(d),
                  pl.BlockSpec((1, 6, d), lambda b, i: (b, 0, 0)),
                  full(norm_g), full(w_nat), full(w_tr), full(gate_w_bd), full(gate_b_cat),
                  full(q_gain), full(k_gain),
                  pl.BlockSpec((tm, 128), lambda b, i: (i, 0)),
                  pl.BlockSpec((tm, 128), lambda b, i: (i, 0)),
                  pl.BlockSpec((GQA_HEAD_DIM, tm), lambda b, i: (0, i)),
                  pl.BlockSpec((GQA_HEAD_DIM, tm), lambda b, i: (0, i))],
        out_specs=[tok(128), tok(128), tok(256), tok(256), tok(128), tok(128), tok(256), tok(256),
                   tok(128), tok(128),
                   pl.BlockSpec((1, GQA_HEADS, GQA_HEAD_DIM, tm), lambda b, i: (b, 0, 0, i)),
                   tok(128),
                   pl.BlockSpec((1, GQA_KV_HEADS, GQA_HEAD_DIM, tm), lambda b, i: (b, 0, 0, i))],
        out_shape=[f32_out(128), f32_out(128), jax.ShapeDtypeStruct((bsz, n, 256), BF16), f32_out(256),
                   f32_out(128), f32_out(128), jax.ShapeDtypeStruct((bsz, n, 256), BF16), f32_out(256),
                   f32_out(128), f32_out(128),
                   jax.ShapeDtypeStruct((bsz, GQA_HEADS, GQA_HEAD_DIM, n), BF16),
                   jax.ShapeDtypeStruct((bsz, n, 128), BF16),
                   jax.ShapeDtypeStruct((bsz, GQA_KV_HEADS, GQA_HEAD_DIM, n), BF16)],
        compiler_params=pltpu.CompilerParams(dimension_semantics=("parallel", "parallel"),
                                             vmem_limit_bytes=V7X_VMEM_LIMIT_BYTES),
        name="in_proj_rope" if rope else "in_proj_ctx",
    )(x, mod, norm_g, w_nat, w_tr, gate_w_bd, gate_b_cat, q_gain, k_gain, cos_r, sin_r, cos_t, sin_t)
    return outs


def _cumsum_rows(g, reverse):
    c = g.shape[0]
    row = lax.broadcasted_iota(jnp.int32, g.shape, 0)
    b = g
    s = 1
    while s < c:
        if reverse:
            b = b + jnp.where(row < c - s, pltpu.roll(b, c - s, 0), 0.0)
        else:
            b = b + jnp.where(row >= s, pltpu.roll(b, s, 0), 0.0)
        s *= 2
    return b


def _scan_masks(c):
    lane_k = lax.broadcasted_iota(jnp.int32, (c, 128), 1) // RET_DK
    lane_v = lax.broadcasted_iota(jnp.int32, (c, 256), 1) // RET_DV
    i = lax.broadcasted_iota(jnp.int32, (c, RET_HEADS * c), 0)
    j = lax.broadcasted_iota(jnp.int32, (c, RET_HEADS * c), 1) % c
    r = lax.broadcasted_iota(jnp.int32, (128, 256), 0) // RET_DK
    l = lax.broadcasted_iota(jnp.int32, (128, 256), 1) // RET_DV
    eye = lax.broadcasted_iota(jnp.int32, (128, 128), 0) == lax.broadcasted_iota(jnp.int32, (128, 128), 1)
    return dict(head_k=[lane_k == h for h in range(RET_HEADS)], head_v=[lane_v == h for h in range(RET_HEADS)],
                keep={False: j <= i, True: j > i}, block_diag=r == l, eye=eye)


def _scan_decays(b, b_end, masks):
    decay_col = jnp.sum(jnp.where(masks["eye"], jnp.exp(b_end), 0.0), axis=1, keepdims=True)
    return jnp.exp(b), jnp.exp(-b), jnp.exp(b_end - b), decay_col


def _scan_chunk_dots(q, k, v, decays, masks):
    e_b, e_nb, e_ub, decay_col = decays
    qd = (q * e_b).astype(BF16)
    kd = (k * e_nb).astype(BF16)
    ku = (k * e_ub).astype(BF16)
    vb = v.astype(BF16)
    zk = jnp.zeros_like(kd)
    zv = jnp.zeros_like(vb)
    kd_stack = jnp.concatenate([jnp.where(m, kd, zk) for m in masks["head_k"]], axis=0)
    v_stack = jnp.concatenate([jnp.where(m, vb, zv) for m in masks["head_v"]], axis=0)
    a = _dot_nt(qd, kd_stack)
    u = _dot_tn(ku, vb)
    return qd, a, v_stack, jnp.where(masks["block_diag"], u, 0.0), decay_col


def _scan_chunk_intra(a, v_stack, keep):
    return _dot(jnp.where(keep, a, 0.0).astype(BF16), v_stack)


def _scan_tile(fwd_refs, bwd_refs, out_refs, st_ref, lg_ref, *, gated, tile):
    c = SCAN_CHUNK
    n_chunks = tile // c
    order = {0: list(range(n_chunks)), 1: list(range(n_chunks - 1, -1, -1))}
    masks = _scan_masks(c)
    local = {}
    for direction in range(2):
        reverse = direction == 1
        q_ref, k_ref, v_ref, g_ref = bwd_refs if reverse else fwd_refs
        if not gated:
            lg = _log_sigmoid(lg_ref[direction:direction + 1, :])
            row = lax.broadcasted_iota(jnp.int32, (c, 128), 0)
            steps = (c - row) if reverse else (row + 1)
            decays = _scan_decays(steps.astype(F32) * lg, float(c) * lg, masks)
        for ci in order[direction]:
            rows = pl.ds(ci * c, c)
            if gated:
                b = _cumsum_rows(g_ref[0, rows, :], reverse)
                decays = _scan_decays(b, b[0:1, :] if reverse else b[c - 1:c, :], masks)
            local[direction, ci] = _scan_chunk_dots(q_ref[0, rows, :], k_ref[0, rows, :], v_ref[0, rows, :],
                                                    decays, masks)

    for direction in range(2):
        for ci in order[direction]:
            qd, a, v_stack, u, decay = local[direction, ci]
            out_refs[direction][0, pl.ds(ci * c, c), :] = _scan_chunk_intra(a, v_stack, masks["keep"][direction == 1])
            local[direction, ci] = (qd, u, decay)

    start_state = {}
    for direction in range(2):
        st = st_ref[direction]
        for ci in order[direction]:
            _, u, decay = local[direction, ci]
            start_state[direction, ci] = st.astype(BF16)
            st = st * decay + u
        st_ref[direction] = st

    for direction in range(2):
        o_ref = out_refs[direction]
        for ci in order[direction]:
            rows = pl.ds(ci * c, c)
            o_ref[0, rows, :] = o_ref[0, rows, :] + _dot(local[direction, ci][0], start_state[direction, ci])


def _scan_kernel(*refs, gated, tile, ctx_tile):
    if gated:
        (qf, kf, vf, gf, qb, kb, vb, gb, qc, kc, vc, gfc, gbc, of, ob, ocf, ocb, st_ref) = refs
        lg_ref = None
    else:
        (qf, kf, vf, qb, kb, vb, qc, kc, vc, lg_ref, of, ob, ocf, ocb, st_ref) = refs
        gf = gb = gfc = gbc = None
    step = pl.program_id(1)
    tile_fn = functools.partial(_scan_tile, st_ref=st_ref, lg_ref=lg_ref, gated=gated)

    @pl.when(step == 0)
    def _():
        st_ref[...] = jnp.zeros_like(st_ref)
        tile_fn((qc, kc, vc, gfc), (qc, kc, vc, gbc), (ocf, ocb), tile=ctx_tile)

    @pl.when(step > 0)
    def _():
        tile_fn((qf, kf, vf, gf), (qb, kb, vb, gb), (of, ob), tile=tile)


def _bidir_scan(lat, ctx, gates_lat, gates_ctx, decay_logit, *, tile):
    q, k, v = lat
    bsz, n_lat, _ = q.shape
    tile = min(tile, n_lat)
    n_lat_tiles = n_lat // tile
    ctx_tile = ctx[0].shape[1]
    assert n_lat % tile == 0 and tile % SCAN_CHUNK == 0 and ctx_tile % SCAN_CHUNK == 0
    fwd = lambda b, s: (b, jnp.maximum(s - 1, 0), 0)
    bwd = lambda b, s: (b, n_lat_tiles - 1 - jnp.maximum(s - 1, 0), 0)
    one = lambda b, s: (b, 0, 0)
    spec = lambda w, im: pl.BlockSpec((1, ctx_tile if im is one else tile, w), im)
    qkv = lambda im: [spec(128, im), spec(128, im), spec(256, im)]
    gated = gates_lat is not None
    if gated:
        args = (q, k, v, gates_lat[0], q, k, v, gates_lat[1], *ctx, *gates_ctx)
        in_specs = (qkv(fwd) + [spec(128, fwd)] + qkv(bwd) + [spec(128, bwd)] + qkv(one)
                    + [spec(128, one), spec(128, one)])
    else:
        args = (q, k, v, q, k, v, *ctx, decay_logit)
        in_specs = qkv(fwd) + qkv(bwd) + qkv(one) + [pl.BlockSpec(decay_logit.shape, lambda b, s: (0, 0))]
    return pl.pallas_call(
        functools.partial(_scan_kernel, gated=gated, tile=tile, ctx_tile=ctx_tile),
        grid=(bsz, n_lat_tiles + 1),
        in_specs=in_specs,
        out_specs=[spec(256, fwd), spec(256, bwd), spec(256, one), spec(256, one)],
        out_shape=([jax.ShapeDtypeStruct((bsz, n_lat, 256), F32)] * 2
                   + [jax.ShapeDtypeStruct((bsz, ctx_tile, 256), F32)] * 2),
        scratch_shapes=[pltpu.VMEM((2, 128, 256), F32)],
        compiler_params=pltpu.CompilerParams(dimension_semantics=("parallel", "arbitrary"),
                                             vmem_limit_bytes=V7X_VMEM_LIMIT_BYTES),
        name="scan_gla" if gated else "scan_ret",
    )(*args)


def _attn_kernel(q_ref, k_ref, v_ref, o_ref, acc_ref, qpad_ref, *, n_kv_tiles):
    kv = pl.program_id(1)
    tq = q_ref.shape[3]
    tk = k_ref.shape[2]
    width = GQA_GROUP * tq
    qcat = jnp.concatenate([q_ref[0, g] for g in range(GQA_GROUP)], axis=1)
    zero = jnp.zeros_like(qcat)
    qpad_ref[...] = jnp.where(kv == 0, jnp.concatenate([qcat, zero], axis=0),
                              jnp.concatenate([zero, qcat], axis=0))
    acc_ref[...] = jnp.zeros_like(acc_ref)

    def body(j, m_all):
        n_cb = width // ATTN_COL_BLOCK
        m_parts = [m_all[:, cb * ATTN_COL_BLOCK:(cb + 1) * ATTN_COL_BLOCK] for cb in range(n_cb)]
        units = [(sub, cb) for sub in range(tk // ATTN_SUB_KEYS) for cb in range(n_cb)]
        rows = lambda sub: slice(sub * ATTN_SUB_KEYS, (sub + 1) * ATTN_SUB_KEYS)
        cols = lambda cb: slice(cb * ATTN_COL_BLOCK, (cb + 1) * ATTN_COL_BLOCK)

        def scores(u):
            sub, cb = units[u]
            return _dot(k_ref[0, j, rows(sub), :], qpad_ref[:, cols(cb)])

        pending = [scores(u) for u in range(min(ATTN_LOOKAHEAD, len(units)))]
        for u, (sub, cb) in enumerate(units):
            if u + ATTN_LOOKAHEAD < len(units):
                pending.append(scores(u + ATTN_LOOKAHEAD))
            s = pending.pop(0).astype(BF16)
            m_new = jnp.maximum(m_parts[cb], jnp.max(s, axis=0, keepdims=True))
            p = jnp.exp2(s - m_new)
            alpha = jnp.exp2(m_parts[cb].astype(F32) - m_new.astype(F32))
            acc_ref[:, cols(cb)] = acc_ref[:, cols(cb)] * alpha + _dot(v_ref[0, 0, j, :, rows(sub)], p)
            m_parts[cb] = m_new
        return jnp.concatenate(m_parts, axis=1)

    lax.fori_loop(0, n_kv_tiles, body, jnp.full((1, width), -jnp.inf, BF16))

    acc = acc_ref[...]
    o = acc[0:GQA_HEAD_DIM] * (1.0 / acc[GQA_HEAD_DIM:GQA_HEAD_DIM + 1])
    o_ref[0] = jnp.concatenate([o[:, g * tq:(g + 1) * tq].T for g in range(GQA_GROUP)], axis=1).astype(o_ref.dtype)


def _attention(q_t, k_nat, v_t, *, tq, tk):
    bsz, _, _, nq = q_t.shape
    ns = k_nat.shape[1]
    tq = min(tq, nq)
    tk = min(tk, ns)
    assert nq % tq == 0 and ns % tk == 0 and tk % ATTN_SUB_KEYS == 0 and (GQA_GROUP * tq) % ATTN_COL_BLOCK == 0
    n_kv_tiles = ns // tk
    k_tiles = k_nat.reshape(bsz, n_kv_tiles, tk, 128)
    ones = jnp.ones((bsz, GQA_KV_HEADS, V_ROWS - GQA_HEAD_DIM, ns), BF16)
    v_tiles = jnp.concatenate([v_t, ones], axis=2).reshape(bsz, GQA_KV_HEADS, V_ROWS, n_kv_tiles, tk)
    v_tiles = v_tiles.transpose(0, 1, 3, 2, 4)
    return pl.pallas_call(
        functools.partial(_attn_kernel, n_kv_tiles=n_kv_tiles),
        grid=(bsz, GQA_KV_HEADS, nq // tq),
        in_specs=[pl.BlockSpec((1, GQA_GROUP, GQA_HEAD_DIM, tq), lambda b, h, i: (b, h, 0, i)),
                  pl.BlockSpec((1, n_kv_tiles, tk, 128), lambda b, h, i: (b, 0, 0, 0)),
                  pl.BlockSpec((1, 1, n_kv_tiles, V_ROWS, tk), lambda b, h, i: (b, h, 0, 0, 0))],
        out_specs=pl.BlockSpec((1, tq, GQA_GROUP * GQA_HEAD_DIM), lambda b, h, i: (b, i, h)),
        out_shape=jax.ShapeDtypeStruct((bsz, nq, GQA_HEADS * GQA_HEAD_DIM), BF16),
        scratch_shapes=[pltpu.VMEM((V_ROWS, GQA_GROUP * tq), F32),
                        pltpu.VMEM((2 * GQA_HEAD_DIM, GQA_GROUP * tq), BF16)],
        compiler_params=pltpu.CompilerParams(dimension_semantics=("parallel", "parallel", "parallel"),
                                             vmem_limit_bytes=V7X_VMEM_LIMIT_BYTES),
        name="gqa_attention",
    )(q_t, k_tiles, v_tiles)


def _group_mean(x, gmat):
    hi = x.astype(BF16)
    lo = (x - hi.astype(F32)).astype(BF16)
    return (_dot(hi, gmat) + _dot(lo, gmat)) * (1.0 / RET_DV)


def _post_kernel(x_ref, mod_ref, rf_ref, rb_ref, rg_ref, gf_ref, gb_ref, gg_ref, ya_ref, wo_ref,
                 g_ref, w1_ref, w2_ref, fg_ref, o_ref, *, ff_tile, final_norm):
    r = lax.broadcasted_iota(jnp.int32, (256, 256), 0) // RET_DV
    c = lax.broadcasted_iota(jnp.int32, (256, 256), 1) // RET_DV
    gmat = jnp.where(r == c, 1.0, 0.0).astype(BF16)
    ro = rf_ref[0] + rb_ref[0]
    rc = ro - _group_mean(ro, gmat)
    y_ret = rg_ref[0] * (rc * lax.rsqrt(_group_mean(rc * rc, gmat) + NORM_EPS))
    go = gf_ref[0] + gb_ref[0]
    y_gla = gg_ref[0] * (go * lax.rsqrt(_group_mean(go * go, gmat) + NORM_EPS))
    y = jnp.concatenate([y_ret.astype(BF16), y_gla.astype(BF16), ya_ref[0]], axis=1)
    x = x_ref[0] + mod_ref[0, 2:3, :] * _dot(y, wo_ref[...])

    ms = jnp.mean(x * x, axis=-1, keepdims=True)
    h = x * lax.rsqrt(ms + NORM_EPS) * g_ref[...]
    hb = (h * (1.0 + mod_ref[0, 4:5, :]) + mod_ref[0, 3:4, :]).astype(BF16)
    d_ff = w1_ref.shape[1]
    acc = jnp.zeros(x.shape, F32)
    for c in range(d_ff // ff_tile):
        cols = slice(c * ff_tile, (c + 1) * ff_tile)
        a = jnp.maximum(_dot(hb, w1_ref[:, cols]), 0.0)
        acc = acc + _dot((a * a).astype(BF16), w2_ref[cols, :])
    y = x + mod_ref[0, 5:6, :] * acc
    if final_norm:
        y = y * lax.rsqrt(jnp.mean(y * y, axis=-1, keepdims=True) + NORM_EPS) * fg_ref[...]
    o_ref[0] = y


def _post(x, mod, scans, gates, y_attn, w_out, norm_g, w1, w2, final_g, *, tm, final_norm):
    bsz, n, d = x.shape
    tm = min(tm, n)
    assert n % tm == 0 and w1.shape[1] % POST_FF_TILE == 0
    tok = lambda w: pl.BlockSpec((1, tm, w), lambda b, i: (b, i, 0))
    full = lambda a: pl.BlockSpec(a.shape, lambda b, i: (0,) * a.ndim, pipeline_mode=pl.Buffered(1))
    rf, rb, gf, gb = scans
    rg, gg = gates
    return pl.pallas_call(
        functools.partial(_post_kernel, ff_tile=POST_FF_TILE, final_norm=final_norm),
        grid=(bsz, n // tm),
        in_specs=[tok(d), pl.BlockSpec((1, 6, d), lambda b, i: (b, 0, 0)),
                  tok(256), tok(256), tok(256), tok(256), tok(256), tok(256), tok(512),
                  full(w_out), full(norm_g), full(w1), full(w2), full(final_g)],
        out_specs=tok(d),
        out_shape=jax.ShapeDtypeStruct((bsz, n, d), F32),
        compiler_params=pltpu.CompilerParams(dimension_semantics=("parallel", "parallel"),
                                             vmem_limit_bytes=V7X_VMEM_LIMIT_BYTES),
        name="post_final" if final_norm else "post",
    )(x, mod, rf, rb, rg, gf, gb, gg, y_attn, w_out, norm_g, w1, w2, final_g)


def _prep_w_in(w):
    d = w.shape[0]
    ret = w[:, 0:768]
    gla = w[:, 768:1536]
    low = w[:, 1536:1568]
    aq = w[:, 1568:2080]
    ak = w[:, 2080:2208]
    av = w[:, 2208:2336]
    w_nat = jnp.concatenate([ret, gla, low, jnp.zeros((d, 128 - 2 * GLA_RANK), w.dtype)], axis=1)
    w_tr = jnp.concatenate([aq, ak, av], axis=1).T
    return w_nat.astype(BF16), w_tr.astype(BF16)


def _rope_tables(n_lat):
    t = jnp.arange(n_lat)
    row = (t // GRID_W).astype(F32)
    col = (t % GRID_W).astype(F32)

    def head_tables(head_dim):
        nf = head_dim // 4
        inv_freq = ROPE_BASE ** (-jnp.arange(nf, dtype=F32) / nf)
        ang_r = row[:, None] * inv_freq
        ang_c = col[:, None] * inv_freq
        cos = jnp.concatenate([jnp.cos(ang_r)] * 2 + [jnp.cos(ang_c)] * 2, axis=1)
        sin = jnp.concatenate([-jnp.sin(ang_r), jnp.sin(ang_r), -jnp.sin(ang_c), jnp.sin(ang_c)], axis=1)
        return cos, sin

    cos_r, sin_r = head_tables(RET_DK)
    cos_r = jnp.tile(cos_r, (1, RET_HEADS))
    sin_r = jnp.tile(sin_r, (1, RET_HEADS))
    cos_a, sin_a = head_tables(GQA_HEAD_DIM)
    return cos_r, sin_r, cos_a.T, sin_a.T


def kernel(x, c, ctx, c_ctx, mod_w, mod_b, attn_norm_g, mlp_norm_g, w_in, w_out, ret_decay_logit,
           gla_gate_w, gla_gate_b, qk_norm_g, mlp_w1, mlp_w2, final_norm_g):
    bsz, n_lat, d = x.shape
    n_ctx = ctx.shape[1]
    depth = mod_w.shape[0]
    scan_tile = 2048
    assert n_lat % scan_tile == 0 and n_ctx % SCAN_CHUNK == 0 and bsz <= 7

    cvec = jnp.zeros((8, d), F32).at[:bsz].set(c).at[bsz].set(c_ctx)
    mods = _modulation(cvec, mod_w, mod_b)
    lat_tables = _rope_tables(n_lat)
    ctx_tables = tuple(jnp.zeros((n_ctx, 128), F32) for _ in range(2)) + \
        tuple(jnp.zeros((GQA_HEAD_DIM, n_ctx), F32) for _ in range(2))
    final_g = final_norm_g.reshape(1, d)

    for i in range(depth):
        need_ctx = i < depth - 1
        mod_lat = mods[i, :bsz].reshape(bsz, 6, d)
        mod_ctx = jnp.broadcast_to(mods[i, bsz].reshape(1, 6, d), (bsz, 6, d))
        w_nat, w_tr = _prep_w_in(w_in[i])
        gw = jnp.zeros((128, 256), F32)
        gw = gw.at[0:GLA_RANK, 0:128].set(gla_gate_w[i, 0]).at[GLA_RANK:2 * GLA_RANK, 128:256].set(gla_gate_w[i, 1])
        gw = gw.astype(BF16)
        gb = gla_gate_b[i].reshape(1, 256)
        q_gain = qk_norm_g[i, 0].reshape(GQA_HEAD_DIM, 1)
        k_gain = qk_norm_g[i, 1].reshape(GQA_HEAD_DIM, 1)
        attn_g = attn_norm_g[i].reshape(1, d)
        mlp_g = mlp_norm_g[i].reshape(1, d)
        decay = jnp.repeat(ret_decay_logit[i], RET_DK, axis=1)
        w_o = w_out[i].astype(BF16)
        w1 = mlp_w1[i].astype(BF16)
        w2 = mlp_w2[i].astype(BF16)

        pl_lat = _in_proj(x, mod_lat, attn_g, w_nat, w_tr, gw, gb, q_gain, k_gain, lat_tables, rope=True, tm=1024)
        pl_ctx = _in_proj(ctx, mod_ctx, attn_g, w_nat, w_tr, gw, gb, q_gain, k_gain, ctx_tables, rope=False,
                          tm=n_ctx)
        ret = _bidir_scan(pl_lat[0:3], pl_ctx[0:3], None, None, decay, tile=scan_tile)
        gla = _bidir_scan(pl_lat[4:7], pl_ctx[4:7], pl_lat[8:10], pl_ctx[8:10], None, tile=scan_tile)
        ak = jnp.concatenate([pl_lat[11], pl_ctx[11]], axis=1)
        av = jnp.concatenate([pl_lat[12], pl_ctx[12]], axis=3)
        y_attn = _attention(pl_lat[10], ak, av, tq=2048, tk=3328)

        x = _post(x, mod_lat, (ret[0], ret[1], gla[0], gla[1]), (pl_lat[3], pl_lat[7]), y_attn, w_o,
                  mlp_g, w1, w2, final_g, tm=512, final_norm=not need_ctx)
        if need_ctx:
            y_attn_c = _attention(pl_ctx[10], pl_ctx[11], pl_ctx[12], tq=256, tk=256)
            ctx = _post(ctx, mod_ctx, (ret[2], ret[3], gla[2], gla[3]), (pl_ctx[3], pl_ctx[7]), y_attn_c, w_o,
                        mlp_g, w1, w2, final_g, tm=n_ctx, final_norm=False)
    return x
```

```python
import functools

import jax
import jax.numpy as jnp
import numpy as np
from jax import lax
from jax.experimental import pallas as pl
from jax.experimental.pallas import tpu as pltpu

F32 = jnp.float32
BF16 = jnp.bfloat16

GRID_W = 64
RET_HEADS = 4
RET_DK = 32
RET_DV = 64
GLA_HEADS = 4
GLA_DK = 32
GLA_DV = 64
GLA_RANK = 16
GLA_TAU = 16.0
GQA_HEADS = 8
GQA_KV_HEADS = 2
GQA_GROUP = 4
GQA_HEAD_DIM = 64
SCAN_CHUNK = 64
ROPE_BASE = 10000.0
NORM_EPS = 1e-6

V7X_VMEM_LIMIT_BYTES = 56 * 1024 * 1024
V_ROWS = 80
POST_FF_TILE = 1024
MOD_COL_TILE = 1536
IN_PROJ_SUB = 1024
ATTN_SUB_KEYS = 256
ATTN_COL_BLOCK = 256
ATTN_LOOKAHEAD = 6
LOG2_E = 1.4426950408889634

NAT_WIDTH = 1664
TR_WIDTH = 768


def _silu(x):
    return x * (1.0 / (1.0 + jnp.exp(-x)))


def _log_sigmoid(z):
    return jnp.minimum(z, 0.0) - jnp.log(1.0 + jnp.exp(-jnp.abs(z)))


def _dot(a, b):
    return jnp.dot(a, b, preferred_element_type=F32)


def _dot_nt(a, b):
    return lax.dot_general(a, b, (((1,), (1,)), ((), ())), preferred_element_type=F32)


def _dot_tn(a, b):
    return lax.dot_general(a, b, (((0,), (0,)), ((), ())), preferred_element_type=F32)


def _mod_kernel(c_ref, w_ref, b_ref, o_ref):
    s = _silu(c_ref[...]).astype(BF16)
    o_ref[0] = _dot(s, w_ref[0].astype(BF16)) + b_ref[0]


def _modulation(cvec, mod_w, mod_b):
    depth, d, n = mod_w.shape
    tn = MOD_COL_TILE
    assert n % tn == 0
    return pl.pallas_call(
        _mod_kernel,
        grid=(depth, n // tn),
        in_specs=[pl.BlockSpec((8, d), lambda l, j: (0, 0)),
                  pl.BlockSpec((1, d, tn), lambda l, j: (l, 0, j)),
                  pl.BlockSpec((1, 1, tn), lambda l, j: (l, 0, j))],
        out_specs=pl.BlockSpec((1, 8, tn), lambda l, j: (l, 0, j)),
        out_shape=jax.ShapeDtypeStruct((depth, 8, n), F32),
        compiler_params=pltpu.CompilerParams(dimension_semantics=("parallel", "parallel"),
                                             vmem_limit_bytes=V7X_VMEM_LIMIT_BYTES),
        name="modulation",
    )(cvec, mod_w, mod_b.reshape(depth, 1, n))


def _rope_nat(x, cos, sin_signed, nf):
    lane = lax.broadcasted_iota(jnp.int32, x.shape, 1)
    first = (lane % (2 * nf)) < nf
    partner = jnp.where(first, pltpu.roll(x, 128 - nf, 1), pltpu.roll(x, nf, 1))
    return x * cos + partner * sin_signed


def _head_norm_rope_t(xt, g_col, cos_t, sin_t, rope):
    h = xt.shape[0] // GQA_HEAD_DIM
    x3 = xt.reshape(h, GQA_HEAD_DIM, xt.shape[1])
    ms = jnp.sum(x3 * x3, axis=1, keepdims=True) * (1.0 / GQA_HEAD_DIM)
    x3 = x3 * lax.rsqrt(ms + NORM_EPS) * g_col[None]
    if rope:
        swapped = jnp.concatenate([x3[:, 16:32], x3[:, 0:16], x3[:, 48:64], x3[:, 32:48]], axis=1)
        x3 = x3 * cos_t[None] + swapped * sin_t[None]
    return x3


def _in_proj_kernel(x_ref, mod_ref, g_ref, wn_ref, wt_ref, gw_ref, gb_ref, qg_ref, kg_ref,
                    cr_ref, sr_ref, ct_ref, st_ref,
                    rq_ref, rk_ref, rv_ref, rg_ref, gq_ref, gk_ref, gv_ref, gg_ref, gf_ref, gbw_ref,
                    aq_ref, ak_ref, av_ref, *, rope, sub):
    n_sub = x_ref.shape[1] // sub
    rows_of = lambda r: slice(r * sub, (r + 1) * sub)

    def project(r):
        x = x_ref[0, rows_of(r), :]
        ms = jnp.mean(x * x, axis=-1, keepdims=True)
        h = x * lax.rsqrt(ms + NORM_EPS) * g_ref[...]
        h = h * (1.0 + mod_ref[0, 1:2, :]) + mod_ref[0, 0:1, :]
        hb = h.astype(BF16)
        return _dot(hb, wn_ref[...]), _dot_nt(wt_ref[...], hb)

    def finish(r, p, pt):
        rows = rows_of(r)
        rq = p[:, 0:128]
        rk = p[:, 128:256]
        gq = p[:, 768:896]
        if rope:
            cr = cr_ref[rows, :]
            sr = sr_ref[rows, :]
            rq = _rope_nat(rq, cr, sr, RET_DK // 4)
            rk = _rope_nat(rk, cr, sr, RET_DK // 4)
        rq_ref[0, rows, :] = rq
        rk_ref[0, rows, :] = rk * (RET_DK ** -0.5)
        rv_ref[0, rows, :] = p[:, 256:512].astype(BF16)
        rg_ref[0, rows, :] = _silu(p[:, 512:768])
        gq_ref[0, rows, :] = gq * (GLA_DK ** -0.5)
        gk_ref[0, rows, :] = p[:, 896:1024]
        gv_ref[0, rows, :] = p[:, 1024:1280].astype(BF16)
        gg_ref[0, rows, :] = _silu(p[:, 1280:1536])
        z = _dot(p[:, 1536:1664].astype(BF16), gw_ref[...]) + gb_ref[...]
        gates = _log_sigmoid(z) * (1.0 / GLA_TAU)
        gf_ref[0, rows, :] = gates[:, 0:128]
        gbw_ref[0, rows, :] = gates[:, 128:256]

        ct = ct_ref[:, rows]
        st = st_ref[:, rows]
        q3 = _head_norm_rope_t(pt[0:512], qg_ref[...], ct, st, rope)
        aq_ref[0, :, :, rows] = (q3 * (GQA_HEAD_DIM ** -0.5 * LOG2_E)).astype(BF16)
        k3 = _head_norm_rope_t(pt[512:640], kg_ref[...], ct, st, rope)
        ak_ref[0, rows, :] = k3.reshape(GQA_KV_HEADS * GQA_HEAD_DIM, sub).T.astype(BF16)
        av_ref[0, :, :, rows] = pt[640:768].reshape(GQA_KV_HEADS, GQA_HEAD_DIM, sub).astype(BF16)

    cur = project(0)
    for r in range(n_sub):
        nxt = project(r + 1) if r + 1 < n_sub else None
        finish(r, *cur)
        cur = nxt


def _in_proj(x, mod, norm_g, w_nat, w_tr, gate_w_bd, gate_b_cat, q_gain, k_gain, tables, *, rope, tm):
    bsz, n, d = x.shape
    tm = min(tm, n)
    assert n % tm == 0 and tm % min(tm, IN_PROJ_SUB) == 0
    cos_r, sin_r, cos_t, sin_t = tables
    tok = lambda w: pl.BlockSpec((1, tm, w), lambda b, i: (b, i, 0))
    full = lambda a: pl.BlockSpec(a.shape, lambda b, i: (0,) * a.ndim)
    f32_out = lambda w: jax.ShapeDtypeStruct((bsz, n, w), F32)
    outs = pl.pallas_call(
        functools.partial(_in_proj_kernel, rope=rope, sub=min(tm, IN_PROJ_SUB)),
        grid=(bsz, n // tm),
        in_specs=[tok(d),
                  pl.BlockSpec((1, 6, d), lambda b, i: (b, 0, 0)),
                  full(norm_g), full(w_nat), full(w_tr), full(gate_w_bd), full(gate_b_cat),
                  full(q_gain), full(k_gain),
                  pl.BlockSpec((tm, 128), lambda b, i: (i, 0)),
                  pl.BlockSpec((tm, 128), lambda b, i: (i, 0)),
                  pl.BlockSpec((GQA_HEAD_DIM, tm), lambda b, i: (0, i)),
                  pl.BlockSpec((GQA_HEAD_DIM, tm), lambda b, i: (0, i))],
        out_specs=[tok(128), tok(128), tok(256), tok(256), tok(128), tok(128), tok(256), tok(256),
                   tok(128), tok(128),
                   pl.BlockSpec((1, GQA_HEADS, GQA_HEAD_DIM, tm), lambda b, i: (b, 0, 0, i)),
                   tok(128),
                   pl.BlockSpec((1, GQA_KV_HEADS, GQA_HEAD_DIM, tm), lambda b, i: (b, 0, 0, i))],
        out_shape=[f32_out(128), f32_out(128), jax.ShapeDtypeStruct((bsz, n, 256), BF16), f32_out(256),
                   f32_out(128), f32_out(128), jax.ShapeDtypeStruct((bsz, n, 256), BF16), f32_out(256),
                   f32_out(128), f32_out(128),
                   jax.ShapeDtypeStruct((bsz, GQA_HEADS, GQA_HEAD_DIM, n), BF16),
                   jax.ShapeDtypeStruct((bsz, n, 128), BF16),
                   jax.ShapeDtypeStruct((bsz, GQA_KV_HEADS, GQA_HEAD_DIM, n), BF16)],
        compiler_params=pltpu.CompilerParams(dimension_semantics=("parallel", "parallel"),
                                             vmem_limit_bytes=V7X_VMEM_LIMIT_BYTES),
        name="in_proj_rope" if rope else "in_proj_ctx",
    )(x, mod, norm_g, w_nat, w_tr, gate_w_bd, gate_b_cat, q_gain, k_gain, cos_r, sin_r, cos_t, sin_t)
    return outs


def _cumsum_rows(g, reverse):
    c = g.shape[0]
    row = lax.broadcasted_iota(jnp.int32, g.shape, 0)
    b = g
    s = 1
    while s < c:
        if reverse:
            b = b + jnp.where(row < c - s, pltpu.roll(b, c - s, 0), 0.0)
        else:
            b = b + jnp.where(row >= s, pltpu.roll(b, s, 0), 0.0)
        s *= 2
    return b


def _scan_masks(c):
    lane_k = lax.broadcasted_iota(jnp.int32, (c, 128), 1) // RET_DK
    lane_v = lax.broadcasted_iota(jnp.int32, (c, 256), 1) // RET_DV
    i = lax.broadcasted_iota(jnp.int32, (c, RET_HEADS * c), 0)
    j = lax.broadcasted_iota(jnp.int32, (c, RET_HEADS * c), 1) % c
    r = lax.broadcasted_iota(jnp.int32, (128, 256), 0) // RET_DK
    l = lax.broadcasted_iota(jnp.int32, (128, 256), 1) // RET_DV
    eye = lax.broadcasted_iota(jnp.int32, (128, 128), 0) == lax.broadcasted_iota(jnp.int32, (128, 128), 1)
    return dict(head_k=[lane_k == h for h in range(RET_HEADS)], head_v=[lane_v == h for h in range(RET_HEADS)],
                keep={False: j <= i, True: j > i}, block_diag=r == l, eye=eye)


def _scan_decays(b, b_end, masks):
    decay_col = jnp.sum(jnp.where(masks["eye"], jnp.exp(b_end), 0.0), axis=1, keepdims=True)
    return jnp.exp(b), jnp.exp(-b), jnp.exp(b_end - b), decay_col


def _scan_chunk_dots(q, k, v, decays, masks):
    e_b, e_nb, e_ub, decay_col = decays
    qd = (q * e_b).astype(BF16)
    kd = (k * e_nb).astype(BF16)
    ku = (k * e_ub).astype(BF16)
    vb = v.astype(BF16)
    zk = jnp.zeros_like(kd)
    zv = jnp.zeros_like(vb)
    kd_stack = jnp.concatenate([jnp.where(m, kd, zk) for m in masks["head_k"]], axis=0)
    v_stack = jnp.concatenate([jnp.where(m, vb, zv) for m in masks["head_v"]], axis=0)
    a = _dot_nt(qd, kd_stack)
    u = _dot_tn(ku, vb)
    return qd, a, v_stack, jnp.where(masks["block_diag"], u, 0.0), decay_col


def _scan_chunk_intra(a, v_stack, keep):
    return _dot(jnp.where(keep, a, 0.0).astype(BF16), v_stack)


def _scan_tile(fwd_refs, bwd_refs, out_refs, st_ref, lg_ref, *, gated, tile):
    c = SCAN_CHUNK
    n_chunks = tile // c
    order = {0: list(range(n_chunks)), 1: list(range(n_chunks - 1, -1, -1))}
    masks = _scan_masks(c)
    local = {}
    for direction in range(2):
        reverse = direction == 1
        q_ref, k_ref, v_ref, g_ref = bwd_refs if reverse else fwd_refs
        if not gated:
            lg = _log_sigmoid(lg_ref[direction:direction + 1, :])
            row = lax.broadcasted_iota(jnp.int32, (c, 128), 0)
            steps = (c - row) if reverse else (row + 1)
            decays = _scan_decays(steps.astype(F32) * lg, float(c) * lg, masks)
        for ci in order[direction]:
            rows = pl.ds(ci * c, c)
            if gated:
                b = _cumsum_rows(g_ref[0, rows, :], reverse)
                decays = _scan_decays(b, b[0:1, :] if reverse else b[c - 1:c, :], masks)
            local[direction, ci] = _scan_chunk_dots(q_ref[0, rows, :], k_ref[0, rows, :], v_ref[0, rows, :],
                                                    decays, masks)

    for direction in range(2):
        for ci in order[direction]:
            qd, a, v_stack, u, decay = local[direction, ci]
            out_refs[direction][0, pl.ds(ci * c, c), :] = _scan_chunk_intra(a, v_stack, masks["keep"][direction == 1])
            local[direction, ci] = (qd, u, decay)

    start_state = {}
    for direction in range(2):
        st = st_ref[direction]
        for ci in order[direction]:
            _, u, decay = local[direction, ci]
            start_state[direction, ci] = st.astype(BF16)
            st = st * decay + u
        st_ref[direction] = st

    for direction in range(2):
        o_ref = out_refs[direction]
        for ci in order[direction]:
            rows = pl.ds(ci * c, c)
            o_ref[0, rows, :] = o_ref[0, rows, :] + _dot(local[direction, ci][0], start_state[direction, ci])


def _scan_kernel(*refs, gated, tile, ctx_tile):
    if gated:
        (qf, kf, vf, gf, qb, kb, vb, gb, qc, kc, vc, gfc, gbc, of, ob, ocf, ocb, st_ref) = refs
        lg_ref = None
    else:
        (qf, kf, vf, qb, kb, vb, qc, kc, vc, lg_ref, of, ob, ocf, ocb, st_ref) = refs
        gf = gb = gfc = gbc = None
    step = pl.program_id(1)
    tile_fn = functools.partial(_scan_tile, st_ref=st_ref, lg_ref=lg_ref, gated=gated)

    @pl.when(step == 0)
    def _():
        st_ref[...] = jnp.zeros_like(st_ref)
        tile_fn((qc, kc, vc, gfc), (qc, kc, vc, gbc), (ocf, ocb), tile=ctx_tile)

    @pl.when(step > 0)
    def _():
        tile_fn((qf, kf, vf, gf), (qb, kb, vb, gb), (of, ob), tile=tile)


def _bidir_scan(lat, ctx, gates_lat, gates_ctx, decay_logit, *, tile):
    q, k, v = lat
    bsz, n_lat, _ = q.shape
    tile = min(tile, n_lat)
    n_lat_tiles = n_lat // tile
    ctx_tile = ctx[0].shape[1]
    assert n_lat % tile == 0 and tile % SCAN_CHUNK == 0 and ctx_tile % SCAN_CHUNK == 0
    fwd = lambda b, s: (b, jnp.maximum(s - 1, 0), 0)
    bwd = lambda b, s: (b, n_lat_tiles - 1 - jnp.maximum(s - 1, 0), 0)
    one = lambda b, s: (b, 0, 0)
    spec = lambda w, im: pl.BlockSpec((1, ctx_tile if im is one else tile, w), im)
    qkv = lambda im: [spec(128, im), spec(128, im), spec(256, im)]
    gated = gates_lat is not None
    if gated:
        args = (q, k, v, gates_lat[0], q, k, v, gates_lat[1], *ctx, *gates_ctx)
        in_specs = (qkv(fwd) + [spec(128, fwd)] + qkv(bwd) + [spec(128, bwd)] + qkv(one)
                    + [spec(128, one), spec(128, one)])
    else:
        args = (q, k, v, q, k, v, *ctx, decay_logit)
        in_specs = qkv(fwd) + qkv(bwd) + qkv(one) + [pl.BlockSpec(decay_logit.shape, lambda b, s: (0, 0))]
    return pl.pallas_call(
        functools.partial(_scan_kernel, gated=gated, tile=tile, ctx_tile=ctx_tile),
        grid=(bsz, n_lat_tiles + 1),
        in_specs=in_specs,
        out_specs=[spec(256, fwd), spec(256, bwd), spec(256, one), spec(256, one)],
        out_shape=([jax.ShapeDtypeStruct((bsz, n_lat, 256), F32)] * 2
                   + [jax.ShapeDtypeStruct((bsz, ctx_tile, 256), F32)] * 2),
        scratch_shapes=[pltpu.VMEM((2, 128, 256), F32)],
        compiler_params=pltpu.CompilerParams(dimension_semantics=("parallel", "arbitrary"),
                                             vmem_limit_bytes=V7X_VMEM_LIMIT_BYTES),
        name="scan_gla" if gated else "scan_ret",
    )(*args)


def _attn_kernel(q_ref, k_ref, v_ref, o_ref, acc_ref, qpad_ref, *, n_kv_tiles):
    kv = pl.program_id(1)
    tq = q_ref.shape[3]
    tk = k_ref.shape[2]
    width = GQA_GROUP * tq
    qcat = jnp.concatenate([q_ref[0, g] for g in range(GQA_GROUP)], axis=1)
    zero = jnp.zeros_like(qcat)
    qpad_ref[...] = jnp.where(kv == 0, jnp.concatenate([qcat, zero], axis=0),
                              jnp.concatenate([zero, qcat], axis=0))
    acc_ref[...] = jnp.zeros_like(acc_ref)

    def body(j, m_all):
        n_cb = width // ATTN_COL_BLOCK
        m_parts = [m_all[:, cb * ATTN_COL_BLOCK:(cb + 1) * ATTN_COL_BLOCK] for cb in range(n_cb)]
        units = [(sub, cb) for sub in range(tk // ATTN_SUB_KEYS) for cb in range(n_cb)]
        rows = lambda sub: slice(sub * ATTN_SUB_KEYS, (sub + 1) * ATTN_SUB_KEYS)
        cols = lambda cb: slice(cb * ATTN_COL_BLOCK, (cb + 1) * ATTN_COL_BLOCK)

        def scores(u):
            sub, cb = units[u]
            return _dot(k_ref[0, j, rows(sub), :], qpad_ref[:, cols(cb)])

        pending = [scores(u) for u in range(min(ATTN_LOOKAHEAD, len(units)))]
        for u, (sub, cb) in enumerate(units):
            if u + ATTN_LOOKAHEAD < len(units):
                pending.append(scores(u + ATTN_LOOKAHEAD))
            s = pending.pop(0).astype(BF16)
            m_new = jnp.maximum(m_parts[cb], jnp.max(s, axis=0, keepdims=True))
            p = jnp.exp2(s - m_new)
            alpha = jnp.exp2(m_parts[cb].astype(F32) - m_new.astype(F32))
            acc_ref[:, cols(cb)] = acc_ref[:, cols(cb)] * alpha + _dot(v_ref[0, 0, j, :, rows(sub)], p)
            m_parts[cb] = m_new
        return jnp.concatenate(m_parts, axis=1)

    lax.fori_loop(0, n_kv_tiles, body, jnp.full((1, width), -jnp.inf, BF16))

    acc = acc_ref[...]
    o = acc[0:GQA_HEAD_DIM] * (1.0 / acc[GQA_HEAD_DIM:GQA_HEAD_DIM + 1])
    o_ref[0] = jnp.concatenate([o[:, g * tq:(g + 1) * tq].T for g in range(GQA_GROUP)], axis=1).astype(o_ref.dtype)


def _attention(q_t, k_nat, v_t, *, tq, tk):
    bsz, _, _, nq = q_t.shape
    ns = k_nat.shape[1]
    tq = min(tq, nq)
    tk = min(tk, ns)
    assert nq % tq == 0 and ns % tk == 0 and tk % ATTN_SUB_KEYS == 0 and (GQA_GROUP * tq) % ATTN_COL_BLOCK == 0
    n_kv_tiles = ns // tk
    k_tiles = k_nat.reshape(bsz, n_kv_tiles, tk, 128)
    ones = jnp.ones((bsz, GQA_KV_HEADS, V_ROWS - GQA_HEAD_DIM, ns), BF16)
    v_tiles = jnp.concatenate([v_t, ones], axis=2).reshape(bsz, GQA_KV_HEADS, V_ROWS, n_kv_tiles, tk)
    v_tiles = v_tiles.transpose(0, 1, 3, 2, 4)
    return pl.pallas_call(
        functools.partial(_attn_kernel, n_kv_tiles=n_kv_tiles),
        grid=(bsz, GQA_KV_HEADS, nq // tq),
        in_specs=[pl.BlockSpec((1, GQA_GROUP, GQA_HEAD_DIM, tq), lambda b, h, i: (b, h, 0, i)),
                  pl.BlockSpec((1, n_kv_tiles, tk, 128), lambda b, h, i: (b, 0, 0, 0)),
                  pl.BlockSpec((1, 1, n_kv_tiles, V_ROWS, tk), lambda b, h, i: (b, h, 0, 0, 0))],
        out_specs=pl.BlockSpec((1, tq, GQA_GROUP * GQA_HEAD_DIM), lambda b, h, i: (b, i, h)),
        out_shape=jax.ShapeDtypeStruct((bsz, nq, GQA_HEADS * GQA_HEAD_DIM), BF16),
        scratch_shapes=[pltpu.VMEM((V_ROWS, GQA_GROUP * tq), F32),
                        pltpu.VMEM((2 * GQA_HEAD_DIM, GQA_GROUP * tq), BF16)],
        compiler_params=pltpu.CompilerParams(dimension_semantics=("parallel", "parallel", "parallel"),
                                             vmem_limit_bytes=V7X_VMEM_LIMIT_BYTES),
        name="gqa_attention",
    )(q_t, k_tiles, v_tiles)


def _group_mean(x, gmat):
    hi = x.astype(BF16)
    lo = (x - hi.astype(F32)).astype(BF16)
    return (_dot(hi, gmat) + _dot(lo, gmat)) * (1.0 / RET_DV)


def _post_kernel(x_ref, mod_ref, rf_ref, rb_ref, rg_ref, gf_ref, gb_ref, gg_ref, ya_ref, wo_ref,
                 g_ref, w1_ref, w2_ref, fg_ref, o_ref, *, ff_tile, final_norm):
    r = lax.broadcasted_iota(jnp.int32, (256, 256), 0) // RET_DV
    c = lax.broadcasted_iota(jnp.int32, (256, 256), 1) // RET_DV
    gmat = jnp.where(r == c, 1.0, 0.0).astype(BF16)
    ro = rf_ref[0] + rb_ref[0]
    rc = ro - _group_mean(ro, gmat)
    y_ret = rg_ref[0] * (rc * lax.rsqrt(_group_mean(rc * rc, gmat) + NORM_EPS))
    go = gf_ref[0] + gb_ref[0]
    y_gla = gg_ref[0] * (go * lax.rsqrt(_group_mean(go * go, gmat) + NORM_EPS))
    y = jnp.concatenate([y_ret.astype(BF16), y_gla.astype(BF16), ya_ref[0]], axis=1)
    x = x_ref[0] + mod_ref[0, 2:3, :] * _dot(y, wo_ref[...])

    ms = jnp.mean(x * x, axis=-1, keepdims=True)
    h = x * lax.rsqrt(ms + NORM_EPS) * g_ref[...]
    hb = (h * (1.0 + mod_ref[0, 4:5, :]) + mod_ref[0, 3:4, :]).astype(BF16)
    d_ff = w1_ref.shape[1]
    acc = jnp.zeros(x.shape, F32)
    for c in range(d_ff // ff_tile):
        cols = slice(c * ff_tile, (c + 1) * ff_tile)
        a = jnp.maximum(_dot(hb, w1_ref[:, cols]), 0.0)
        acc = acc + _dot((a * a).astype(BF16), w2_ref[cols, :])
    y = x + mod_ref[0, 5:6, :] * acc
    if final_norm:
        y = y * lax.rsqrt(jnp.mean(y * y, axis=-1, keepdims=True) + NORM_EPS) * fg_ref[...]
    o_ref[0] = y


def _post(x, mod, scans, gates, y_attn, w_out, norm_g, w1, w2, final_g, *, tm, final_norm):
    bsz, n, d = x.shape
    tm = min(tm, n)
    assert n % tm == 0 and w1.shape[1] % POST_FF_TILE == 0
    tok = lambda w: pl.BlockSpec((1, tm, w), lambda b, i: (b, i, 0))
    full = lambda a: pl.BlockSpec(a.shape, lambda b, i: (0,) * a.ndim, pipeline_mode=pl.Buffered(1))
    rf, rb, gf, gb = scans
    rg, gg = gates
    return pl.pallas_call(
        functools.partial(_post_kernel, ff_tile=POST_FF_TILE, final_norm=final_norm),
        grid=(bsz, n // tm),
        in_specs=[tok(d), pl.BlockSpec((1, 6, d), lambda b, i: (b, 0, 0)),
                  tok(256), tok(256), tok(256), tok(256), tok(256), tok(256), tok(512),
                  full(w_out), full(norm_g), full(w1), full(w2), full(final_g)],
        out_specs=tok(d),
        out_shape=jax.ShapeDtypeStruct((bsz, n, d), F32),
        compiler_params=pltpu.CompilerParams(dimension_semantics=("parallel", "parallel"),
                                             vmem_limit_bytes=V7X_VMEM_LIMIT_BYTES),
        name="post_final" if final_norm else "post",
    )(x, mod, rf, rb, rg, gf, gb, gg, y_attn, w_out, norm_g, w1, w2, final_g)


def _prep_w_in(w):
    d = w.shape[0]
    ret = w[:, 0:768]
    gla = w[:, 768:1536]
    low = w[:, 1536:1568]
    aq = w[:, 1568:2080]
    ak = w[:, 2080:2208]
    av = w[:, 2208:2336]
    w_nat = jnp.concatenate([ret, gla, low, jnp.zeros((d, 128 - 2 * GLA_RANK), w.dtype)], axis=1)
    w_tr = jnp.concatenate([aq, ak, av], axis=1).T
    return w_nat.astype(BF16), w_tr.astype(BF16)


def _rope_tables(n_lat):
    t = jnp.arange(n_lat)
    row = (t // GRID_W).astype(F32)
    col = (t % GRID_W).astype(F32)

    def head_tables(head_dim):
        nf = head_dim // 4
        inv_freq = ROPE_BASE ** (-jnp.arange(nf, dtype=F32) / nf)
        ang_r = row[:, None] * inv_freq
        ang_c = col[:, None] * inv_freq
        cos = jnp.concatenate([jnp.cos(ang_r)] * 2 + [jnp.cos(ang_c)] * 2, axis=1)
        sin = jnp.concatenate([-jnp.sin(ang_r), jnp.sin(ang_r), -jnp.sin(ang_c), jnp.sin(ang_c)], axis=1)
        return cos, sin

    cos_r, sin_r = head_tables(RET_DK)
    cos_r = jnp.tile(cos_r, (1, RET_HEADS))
    sin_r = jnp.tile(sin_r, (1, RET_HEADS))
    cos_a, sin_a = head_tables(GQA_HEAD_DIM)
    return cos_r, sin_r, cos_a.T, sin_a.T


def kernel(x, c, ctx, c_ctx, mod_w, mod_b, attn_norm_g, mlp_norm_g, w_in, w_out, ret_decay_logit,
           gla_gate_w, gla_gate_b, qk_norm_g, mlp_w1, mlp_w2, final_norm_g):
    bsz, n_lat, d = x.shape
    n_ctx = ctx.shape[1]
    depth = mod_w.shape[0]
    scan_tile = 2048
    assert n_lat % scan_tile == 0 and n_ctx % SCAN_CHUNK == 0 and bsz <= 7

    cvec = jnp.zeros((8, d), F32).at[:bsz].set(c).at[bsz].set(c_ctx)
    mods = _modulation(cvec, mod_w, mod_b)
    lat_tables = _rope_tables(n_lat)
    ctx_tables = tuple(jnp.zeros((n_ctx, 128), F32) for _ in range(2)) + \
        tuple(jnp.zeros((GQA_HEAD_DIM, n_ctx), F32) for _ in range(2))
    final_g = final_norm_g.reshape(1, d)

    for i in range(depth):
        need_ctx = i < depth - 1
        mod_lat = mods[i, :bsz].reshape(bsz, 6, d)
        mod_ctx = jnp.broadcast_to(mods[i, bsz].reshape(1, 6, d), (bsz, 6, d))
        w_nat, w_tr = _prep_w_in(w_in[i])
        gw = jnp.zeros((128, 256), F32)
        gw = gw.at[0:GLA_RANK, 0:128].set(gla_gate_w[i, 0]).at[GLA_RANK:2 * GLA_RANK, 128:256].set(gla_gate_w[i, 1])
        gw = gw.astype(BF16)
        gb = gla_gate_b[i].reshape(1, 256)
        q_gain = qk_norm_g[i, 0].reshape(GQA_HEAD_DIM, 1)
        k_gain = qk_norm_g[i, 1].reshape(GQA_HEAD_DIM, 1)
        attn_g = attn_norm_g[i].reshape(1, d)
        mlp_g = mlp_norm_g[i].reshape(1, d)
        decay = jnp.repeat(ret_decay_logit[i], RET_DK, axis=1)
        w_o = w_out[i].astype(BF16)
        w1 = mlp_w1[i].astype(BF16)
        w2 = mlp_w2[i].astype(BF16)

        pl_lat = _in_proj(x, mod_lat, attn_g, w_nat, w_tr, gw, gb, q_gain, k_gain, lat_tables, rope=True, tm=1024)
        pl_ctx = _in_proj(ctx, mod_ctx, attn_g, w_nat, w_tr, gw, gb, q_gain, k_gain, ctx_tables, rope=False,
                          tm=n_ctx)
        ret = _bidir_scan(pl_lat[0:3], pl_ctx[0:3], None, None, decay, tile=scan_tile)
        gla = _bidir_scan(pl_lat[4:7], pl_ctx[4:7], pl_lat[8:10], pl_ctx[8:10], None, tile=scan_tile)
        ak = jnp.concatenate([pl_lat[11], pl_ctx[11]], axis=1)
        av = jnp.concatenate([pl_lat[12], pl_ctx[12]], axis=3)
        y_attn = _attention(pl_lat[10], ak, av, tq=2048, tk=3328)

        x = _post(x, mod_lat, (ret[0], ret[1], gla[0], gla[1]), (pl_lat[3], pl_lat[7]), y_attn, w_o,
                  mlp_g, w1, w2, final_g, tm=512, final_norm=not need_ctx)
        if need_ctx:
            y_attn_c = _attention(pl_ctx[10], pl_ctx[11], pl_ctx[12], tq=256, tk=256)
            ctx = _post(ctx, mod_ctx, (ret[2], ret[3], gla[2], gla[3]), (pl_ctx[3], pl_ctx[7]), y_attn_c, w_o,
                        mlp_g, w1, w2, final_g, tm=n_ctx, final_norm=False)
    return x
```

```python
import functools

import jax
import jax.numpy as jnp
import numpy as np
from jax import lax
from jax.experimental import pallas as pl
from jax.experimental.pallas import tpu as pltpu

F32 = jnp.float32
BF16 = jnp.bfloat16

GRID_W = 64
RET_HEADS = 4
RET_DK = 32
RET_DV = 64
GLA_HEADS = 4
GLA_DK = 32
GLA_DV = 64
GLA_RANK = 16
GLA_TAU = 16.0
GQA_HEADS = 8
GQA_KV_HEADS = 2
GQA_GROUP = 4
GQA_HEAD_DIM = 64
SCAN_CHUNK = 64
ROPE_BASE = 10000.0
NORM_EPS = 1e-6

V7X_VMEM_LIMIT_BYTES = 56 * 1024 * 1024
V_ROWS = 80
POST_FF_TILE = 4096
MOD_COL_TILE = 1536
IN_PROJ_SUB = 1024
ATTN_SUB_KEYS = 256
ATTN_COL_BLOCK = 256
ATTN_LOOKAHEAD = 5
LOG2_E = 1.4426950408889634

NAT_WIDTH = 1664
TR_WIDTH = 768


def _silu(x):
    return x * (1.0 / (1.0 + jnp.exp(-x)))


def _log_sigmoid(z):
    return jnp.minimum(z, 0.0) - jnp.log(1.0 + jnp.exp(-jnp.abs(z)))


def _dot(a, b):
    return jnp.dot(a, b, preferred_element_type=F32)


def _dot_nt(a, b):
    return lax.dot_general(a, b, (((1,), (1,)), ((), ())), preferred_element_type=F32)


def _dot_tn(a, b):
    return lax.dot_general(a, b, (((0,), (0,)), ((), ())), preferred_element_type=F32)


def _mod_kernel(c_ref, w_ref, b_ref, o_ref):
    s = _silu(c_ref[...]).astype(BF16)
    o_ref[0] = _dot(s, w_ref[0].astype(BF16)) + b_ref[0]


def _modulation(cvec, mod_w, mod_b):
    depth, d, n = mod_w.shape
    tn = MOD_COL_TILE
    assert n % tn == 0
    return pl.pallas_call(
        _mod_kernel,
        grid=(depth, n // tn),
        in_specs=[pl.BlockSpec((8, d), lambda l, j: (0, 0)),
                  pl.BlockSpec((1, d, tn), lambda l, j: (l, 0, j)),
                  pl.BlockSpec((1, 1, tn), lambda l, j: (l, 0, j))],
        out_specs=pl.BlockSpec((1, 8, tn), lambda l, j: (l, 0, j)),
        out_shape=jax.ShapeDtypeStruct((depth, 8, n), F32),
        compiler_params=pltpu.CompilerParams(dimension_semantics=("parallel", "parallel"),
                                             vmem_limit_bytes=V7X_VMEM_LIMIT_BYTES),
        name="modulation",
    )(cvec, mod_w, mod_b.reshape(depth, 1, n))


def _rope_nat(x, cos, sin_signed, nf):
    lane = lax.broadcasted_iota(jnp.int32, x.shape, 1)
    first = (lane % (2 * nf)) < nf
    partner = jnp.where(first, pltpu.roll(x, 128 - nf, 1), pltpu.roll(x, nf, 1))
    return x * cos + partner * sin_signed


def _head_norm_rope_t(xt, g_col, cos_t, sin_t, rope):
    h = xt.shape[0] // GQA_HEAD_DIM
    x3 = xt.reshape(h, GQA_HEAD_DIM, xt.shape[1])
    ms = jnp.sum(x3 * x3, axis=1, keepdims=True) * (1.0 / GQA_HEAD_DIM)
    x3 = x3 * lax.rsqrt(ms + NORM_EPS) * g_col[None]
    if rope:
        swapped = jnp.concatenate([x3[:, 16:32], x3[:, 0:16], x3[:, 48:64], x3[:, 32:48]], axis=1)
        x3 = x3 * cos_t[None] + swapped * sin_t[None]
    return x3


def _in_proj_kernel(x_ref, mod_ref, g_ref, wn_ref, wt_ref, gw_ref, gb_ref, qg_ref, kg_ref,
                    cr_ref, sr_ref, ct_ref, st_ref,
                    rq_ref, rk_ref, rv_ref, rg_ref, gq_ref, gk_ref, gv_ref, gg_ref, gf_ref, gbw_ref,
                    aq_ref, ak_ref, av_ref, *, rope, sub):
    n_sub = x_ref.shape[1] // sub
    rows_of = lambda r: slice(r * sub, (r + 1) * sub)

    def project(r):
        x = x_ref[0, rows_of(r), :]
        ms = jnp.mean(x * x, axis=-1, keepdims=True)
        h = x * lax.rsqrt(ms + NORM_EPS) * g_ref[...]
        h = h * (1.0 + mod_ref[0, 1:2, :]) + mod_ref[0, 0:1, :]
        hb = h.astype(BF16)
        return _dot(hb, wn_ref[...]), _dot_nt(wt_ref[...], hb)

    def finish(r, p, pt):
        rows = rows_of(r)
        rq = p[:, 0:128]
        rk = p[:, 128:256]
        gq = p[:, 768:896]
        if rope:
            cr = cr_ref[rows, :]
            sr = sr_ref[rows, :]
            rq = _rope_nat(rq, cr, sr, RET_DK // 4)
            rk = _rope_nat(rk, cr, sr, RET_DK // 4)
        rq_ref[0, rows, :] = rq
        rk_ref[0, rows, :] = rk * (RET_DK ** -0.5)
        rv_ref[0, rows, :] = p[:, 256:512].astype(BF16)
        rg_ref[0, rows, :] = _silu(p[:, 512:768])
        gq_ref[0, rows, :] = gq * (GLA_DK ** -0.5)
        gk_ref[0, rows, :] = p[:, 896:1024]
        gv_ref[0, rows, :] = p[:, 1024:1280].astype(BF16)
        gg_ref[0, rows, :] = _silu(p[:, 1280:1536])
        z = _dot(p[:, 1536:1664].astype(BF16), gw_ref[...]) + gb_ref[...]
        gates = _log_sigmoid(z) * (1.0 / GLA_TAU)
        gf_ref[0, rows, :] = gates[:, 0:128]
        gbw_ref[0, rows, :] = gates[:, 128:256]

        ct = ct_ref[:, rows]
        st = st_ref[:, rows]
        q3 = _head_norm_rope_t(pt[0:512], qg_ref[...], ct, st, rope)
        aq_ref[0, :, :, rows] = (q3 * (GQA_HEAD_DIM ** -0.5 * LOG2_E)).astype(BF16)
        k3 = _head_norm_rope_t(pt[512:640], kg_ref[...], ct, st, rope)
        ak_ref[0, rows, :] = k3.reshape(GQA_KV_HEADS * GQA_HEAD_DIM, sub).T.astype(BF16)
        av_ref[0, :, :, rows] = pt[640:768].reshape(GQA_KV_HEADS, GQA_HEAD_DIM, sub).astype(BF16)

    cur = project(0)
    for r in range(n_sub):
        nxt = project(r + 1) if r + 1 < n_sub else None
        finish(r, *cur)
        cur = nxt


def _in_proj(x, mod, norm_g, w_nat, w_tr, gate_w_bd, gate_b_cat, q_gain, k_gain, tables, *, rope, tm):
    bsz, n, d = x.shape
    tm = min(tm, n)
    assert n % tm == 0 and tm % min(tm, IN_PROJ_SUB) == 0
    cos_r, sin_r, cos_t, sin_t = tables
    tok = lambda w: pl.BlockSpec((1, tm, w), lambda b, i: (b, i, 0))
    full = lambda a: pl.BlockSpec(a.shape, lambda b, i: (0,) * a.ndim)
    f32_out = lambda w: jax.ShapeDtypeStruct((bsz, n, w), F32)
    outs = pl.pallas_call(
        functools.partial(_in_proj_kernel, rope=rope, sub=min(tm, IN_PROJ_SUB)),
        grid=(bsz, n // tm),
        in_specs=[tok(d),
                  pl.BlockSpec((1, 6, d), lambda b, i: (b, 0, 0)),
                  full(norm_g), full(w_nat), full(w_tr), full(gate_w_bd), full(gate_b_cat),
                  full(q_gain), full(k_gain),
                  pl.BlockSpec((tm, 128), lambda b, i: (i, 0)),
                  pl.BlockSpec((tm, 128), lambda b, i: (i, 0)),
                  pl.BlockSpec((GQA_HEAD_DIM, tm), lambda b, i: (0, i)),
                  pl.BlockSpec((GQA_HEAD_DIM, tm), lambda b, i: (0, i))],
        out_specs=[tok(128), tok(128), tok(256), tok(256), tok(128), tok(128), tok(256), tok(256),
                   tok(128), tok(128),
                   pl.BlockSpec((1, GQA_HEADS, GQA_HEAD_DIM, tm), lambda b, i: (b, 0, 0, i)),
                   tok(128),
                   pl.BlockSpec((1, GQA_KV_HEADS, GQA_HEAD_DIM, tm), lambda b, i: (b, 0, 0, i))],
        out_shape=[f32_out(128), f32_out(128), jax.ShapeDtypeStruct((bsz, n, 256), BF16), f32_out(256),
                   f32_out(128), f32_out(128), jax.ShapeDtypeStruct((bsz, n, 256), BF16), f32_out(256),
                   f32_out(128), f32_out(128),
                   jax.ShapeDtypeStruct((bsz, GQA_HEADS, GQA_HEAD_DIM, n), BF16),
                   jax.ShapeDtypeStruct((bsz, n, 128), BF16),
                   jax.ShapeDtypeStruct((bsz, GQA_KV_HEADS, GQA_HEAD_DIM, n), BF16)],
        compiler_params=pltpu.CompilerParams(dimension_semantics=("parallel", "parallel"),
                                             vmem_limit_bytes=V7X_VMEM_LIMIT_BYTES),
        name="in_proj_rope" if rope else "in_proj_ctx",
    )(x, mod, norm_g, w_nat, w_tr, gate_w_bd, gate_b_cat, q_gain, k_gain, cos_r, sin_r, cos_t, sin_t)
    return outs


def _cumsum_rows(g, reverse):
    c = g.shape[0]
    row = lax.broadcasted_iota(jnp.int32, g.shape, 0)
    b = g
    s = 1
    while s < c:
        if reverse:
            b = b + jnp.where(row < c - s, pltpu.roll(b, c - s, 0), 0.0)
        else:
            b = b + jnp.where(row >= s, pltpu.roll(b, s, 0), 0.0)
        s *= 2
    return b


def _scan_masks(c):
    lane_k = lax.broadcasted_iota(jnp.int32, (c, 128), 1) // RET_DK
    lane_v = lax.broadcasted_iota(jnp.int32, (c, 256), 1) // RET_DV
    i = lax.broadcasted_iota(jnp.int32, (c, RET_HEADS * c), 0)
    j = lax.broadcasted_iota(jnp.int32, (c, RET_HEADS * c), 1) % c
    r = lax.broadcasted_iota(jnp.int32, (128, 256), 0) // RET_DK
    l = lax.broadcasted_iota(jnp.int32, (128, 256), 1) // RET_DV
    eye = lax.broadcasted_iota(jnp.int32, (128, 128), 0) == lax.broadcasted_iota(jnp.int32, (128, 128), 1)
    return dict(head_k=[lane_k == h for h in range(RET_HEADS)], head_v=[lane_v == h for h in range(RET_HEADS)],
                keep={False: j <= i, True: j > i}, block_diag=r == l, eye=eye)


def _scan_decays(b, b_end, masks):
    decay_col = jnp.sum(jnp.where(masks["eye"], jnp.exp(b_end), 0.0), axis=1, keepdims=True)
    return jnp.exp(b), jnp.exp(-b), jnp.exp(b_end - b), decay_col


def _scan_chunk_dots(q, k, v, decays, masks):
    e_b, e_nb, e_ub, decay_col = decays
    qd = (q * e_b).astype(BF16)
    kd = (k * e_nb).astype(BF16)
    ku = (k * e_ub).astype(BF16)
    vb = v.astype(BF16)
    zk = jnp.zeros_like(kd)
    zv = jnp.zeros_like(vb)
    kd_stack = jnp.concatenate([jnp.where(m, kd, zk) for m in masks["head_k"]], axis=0)
    v_stack = jnp.concatenate([jnp.where(m, vb, zv) for m in masks["head_v"]], axis=0)
    a = _dot_nt(qd, kd_stack)
    u = _dot_tn(ku, vb)
    return qd, a, v_stack, jnp.where(masks["block_diag"], u, 0.0), decay_col


def _scan_chunk_intra(a, v_stack, keep):
    return _dot(jnp.where(keep, a, 0.0).astype(BF16), v_stack)


def _scan_tile(fwd_refs, bwd_refs, out_refs, st_ref, lg_ref, *, gated, tile):
    c = SCAN_CHUNK
    n_chunks = tile // c
    order = {0: list(range(n_chunks)), 1: list(range(n_chunks - 1, -1, -1))}
    masks = _scan_masks(c)
    local = {}
    for direction in range(2):
        reverse = direction == 1
        q_ref, k_ref, v_ref, g_ref = bwd_refs if reverse else fwd_refs
        if not gated:
            lg = _log_sigmoid(lg_ref[direction:direction + 1, :])
            row = lax.broadcasted_iota(jnp.int32, (c, 128), 0)
            steps = (c - row) if reverse else (row + 1)
            decays = _scan_decays(steps.astype(F32) * lg, float(c) * lg, masks)
        for ci in order[direction]:
            rows = pl.ds(ci * c, c)
            if gated:
                b = _cumsum_rows(g_ref[0, rows, :], reverse)
                decays = _scan_decays(b, b[0:1, :] if reverse else b[c - 1:c, :], masks)
            local[direction, ci] = _scan_chunk_dots(q_ref[0, rows, :], k_ref[0, rows, :], v_ref[0, rows, :],
                                                    decays, masks)

    for direction in range(2):
        for ci in order[direction]:
            qd, a, v_stack, u, decay = local[direction, ci]
            out_refs[direction][0, pl.ds(ci * c, c), :] = _scan_chunk_intra(a, v_stack, masks["keep"][direction == 1])
            local[direction, ci] = (qd, u, decay)

    start_state = {}
    for direction in range(2):
        st = st_ref[direction]
        for ci in order[direction]:
            _, u, decay = local[direction, ci]
            start_state[direction, ci] = st.astype(BF16)
            st = st * decay + u
        st_ref[direction] = st

    for direction in range(2):
        o_ref = out_refs[direction]
        for ci in order[direction]:
            rows = pl.ds(ci * c, c)
            o_ref[0, rows, :] = o_ref[0, rows, :] + _dot(local[direction, ci][0], start_state[direction, ci])


def _scan_kernel(*refs, gated, tile, ctx_tile):
    if gated:
        (qf, kf, vf, gf, qb, kb, vb, gb, qc, kc, vc, gfc, gbc, of, ob, ocf, ocb, st_ref) = refs
        lg_ref = None
    else:
        (qf, kf, vf, qb, kb, vb, qc, kc, vc, lg_ref, of, ob, ocf, ocb, st_ref) = refs
        gf = gb = gfc = gbc = None
    step = pl.program_id(1)
    tile_fn = functools.partial(_scan_tile, st_ref=st_ref, lg_ref=lg_ref, gated=gated)

    @pl.when(step == 0)
    def _():
        st_ref[...] = jnp.zeros_like(st_ref)
        tile_fn((qc, kc, vc, gfc), (qc, kc, vc, gbc), (ocf, ocb), tile=ctx_tile)

    @pl.when(step > 0)
    def _():
        tile_fn((qf, kf, vf, gf), (qb, kb, vb, gb), (of, ob), tile=tile)


def _bidir_scan(lat, ctx, gates_lat, gates_ctx, decay_logit, *, tile):
    q, k, v = lat
    bsz, n_lat, _ = q.shape
    tile = min(tile, n_lat)
    n_lat_tiles = n_lat // tile
    ctx_tile = ctx[0].shape[1]
    assert n_lat % tile == 0 and tile % SCAN_CHUNK == 0 and ctx_tile % SCAN_CHUNK == 0
    fwd = lambda b, s: (b, jnp.maximum(s - 1, 0), 0)
    bwd = lambda b, s: (b, n_lat_tiles - 1 - jnp.maximum(s - 1, 0), 0)
    one = lambda b, s: (b, 0, 0)
    spec = lambda w, im: pl.BlockSpec((1, ctx_tile if im is one else tile, w), im)
    qkv = lambda im: [spec(128, im), spec(128, im), spec(256, im)]
    gated = gates_lat is not None
    if gated:
        args = (q, k, v, gates_lat[0], q, k, v, gates_lat[1], *ctx, *gates_ctx)
        in_specs = (qkv(fwd) + [spec(128, fwd)] + qkv(bwd) + [spec(128, bwd)] + qkv(one)
                    + [spec(128, one), spec(128, one)])
    else:
        args = (q, k, v, q, k, v, *ctx, decay_logit)
        in_specs = qkv(fwd) + qkv(bwd) + qkv(one) + [pl.BlockSpec(decay_logit.shape, lambda b, s: (0, 0))]
    return pl.pallas_call(
        functools.partial(_scan_kernel, gated=gated, tile=tile, ctx_tile=ctx_tile),
        grid=(bsz, n_lat_tiles + 1),
        in_specs=in_specs,
        out_specs=[spec(256, fwd), spec(256, bwd), spec(256, one), spec(256, one)],
        out_shape=([jax.ShapeDtypeStruct((bsz, n_lat, 256), F32)] * 2
                   + [jax.ShapeDtypeStruct((bsz, ctx_tile, 256), F32)] * 2),
        scratch_shapes=[pltpu.VMEM((2, 128, 256), F32)],
        compiler_params=pltpu.CompilerParams(dimension_semantics=("parallel", "arbitrary"),
                                             vmem_limit_bytes=V7X_VMEM_LIMIT_BYTES),
        name="scan_gla" if gated else "scan_ret",
    )(*args)


def _attn_kernel(q_ref, k_ref, v_ref, o_ref, acc_ref, qpad_ref, *, n_kv_tiles):
    kv = pl.program_id(1)
    tq = q_ref.shape[3]
    tk = k_ref.shape[2]
    width = GQA_GROUP * tq
    qcat = jnp.concatenate([q_ref[0, g] for g in range(GQA_GROUP)], axis=1)
    zero = jnp.zeros_like(qcat)
    qpad_ref[...] = jnp.where(kv == 0, jnp.concatenate([qcat, zero], axis=0),
                              jnp.concatenate([zero, qcat], axis=0))
    acc_ref[...] = jnp.zeros_like(acc_ref)

    def body(j, m_all):
        n_cb = width // ATTN_COL_BLOCK
        m_parts = [m_all[:, cb * ATTN_COL_BLOCK:(cb + 1) * ATTN_COL_BLOCK] for cb in range(n_cb)]
        units = [(sub, cb) for sub in range(tk // ATTN_SUB_KEYS) for cb in range(n_cb)]
        rows = lambda sub: slice(sub * ATTN_SUB_KEYS, (sub + 1) * ATTN_SUB_KEYS)
        cols = lambda cb: slice(cb * ATTN_COL_BLOCK, (cb + 1) * ATTN_COL_BLOCK)

        def scores(u):
            sub, cb = units[u]
            return _dot(k_ref[0, j, rows(sub), :], qpad_ref[:, cols(cb)])

        pending = [scores(u) for u in range(min(ATTN_LOOKAHEAD, len(units)))]
        for u, (sub, cb) in enumerate(units):
            if u + ATTN_LOOKAHEAD < len(units):
                pending.append(scores(u + ATTN_LOOKAHEAD))
            s = pending.pop(0).astype(BF16)
            m_new = jnp.maximum(m_parts[cb], jnp.max(s, axis=0, keepdims=True))
            p = jnp.exp2(s - m_new)
            alpha = jnp.exp2(m_parts[cb].astype(F32) - m_new.astype(F32))
            acc_ref[:, cols(cb)] = acc_ref[:, cols(cb)] * alpha + _dot(v_ref[0, 0, j, :, rows(sub)], p)
            m_parts[cb] = m_new
        return jnp.concatenate(m_parts, axis=1)

    lax.fori_loop(0, n_kv_tiles, body, jnp.full((1, width), -jnp.inf, BF16))

    acc = acc_ref[...]
    o = acc[0:GQA_HEAD_DIM] * (1.0 / acc[GQA_HEAD_DIM:GQA_HEAD_DIM + 1])
    o_ref[0] = jnp.concatenate([o[:, g * tq:(g + 1) * tq].T for g in range(GQA_GROUP)], axis=1).astype(o_ref.dtype)


def _attention(q_t, k_nat, v_t, *, tq, tk):
    bsz, _, _, nq = q_t.shape
    ns = k_nat.shape[1]
    tq = min(tq, nq)
    tk = min(tk, ns)
    assert nq % tq == 0 and ns % tk == 0 and tk % ATTN_SUB_KEYS == 0 and (GQA_GROUP * tq) % ATTN_COL_BLOCK == 0
    n_kv_tiles = ns // tk
    k_tiles = k_nat.reshape(bsz, n_kv_tiles, tk, 128)
    ones = jnp.ones((bsz, GQA_KV_HEADS, V_ROWS - GQA_HEAD_DIM, ns), BF16)
    v_tiles = jnp.concatenate([v_t, ones], axis=2).reshape(bsz, GQA_KV_HEADS, V_ROWS, n_kv_tiles, tk)
    v_tiles = v_tiles.transpose(0, 1, 3, 2, 4)
    return pl.pallas_call(
        functools.partial(_attn_kernel, n_kv_tiles=n_kv_tiles),
        grid=(bsz, GQA_KV_HEADS, nq // tq),
        in_specs=[pl.BlockSpec((1, GQA_GROUP, GQA_HEAD_DIM, tq), lambda b, h, i: (b, h, 0, i)),
                  pl.BlockSpec((1, n_kv_tiles, tk, 128), lambda b, h, i: (b, 0, 0, 0)),
                  pl.BlockSpec((1, 1, n_kv_tiles, V_ROWS, tk), lambda b, h, i: (b, h, 0, 0, 0))],
        out_specs=pl.BlockSpec((1, tq, GQA_GROUP * GQA_HEAD_DIM), lambda b, h, i: (b, i, h)),
        out_shape=jax.ShapeDtypeStruct((bsz, nq, GQA_HEADS * GQA_HEAD_DIM), BF16),
        scratch_shapes=[pltpu.VMEM((V_ROWS, GQA_GROUP * tq), F32),
                        pltpu.VMEM((2 * GQA_HEAD_DIM, GQA_GROUP * tq), BF16)],
        compiler_params=pltpu.CompilerParams(dimension_semantics=("parallel", "parallel", "parallel"),
                                             vmem_limit_bytes=V7X_VMEM_LIMIT_BYTES),
        name="gqa_attention",
    )(q_t, k_tiles, v_tiles)


def _group_mean(x, gmat):
    hi = x.astype(BF16)
    lo = (x - hi.astype(F32)).astype(BF16)
    return (_dot(hi, gmat) + _dot(lo, gmat)) * (1.0 / RET_DV)


def _post_kernel(x_ref, mod_ref, rf_ref, rb_ref, rg_ref, gf_ref, gb_ref, gg_ref, ya_ref, wo_ref,
                 g_ref, w1_ref, w2_ref, fg_ref, o_ref, *, ff_tile, final_norm):
    r = lax.broadcasted_iota(jnp.int32, (256, 256), 0) // RET_DV
    c = lax.broadcasted_iota(jnp.int32, (256, 256), 1) // RET_DV
    gmat = jnp.where(r == c, 1.0, 0.0).astype(BF16)
    ro = rf_ref[0] + rb_ref[0]
    rc = ro - _group_mean(ro, gmat)
    y_ret = rg_ref[0] * (rc * lax.rsqrt(_group_mean(rc * rc, gmat) + NORM_EPS))
    go = gf_ref[0] + gb_ref[0]
    y_gla = gg_ref[0] * (go * lax.rsqrt(_group_mean(go * go, gmat) + NORM_EPS))
    y = jnp.concatenate([y_ret.astype(BF16), y_gla.astype(BF16), ya_ref[0]], axis=1)
    x = x_ref[0] + mod_ref[0, 2:3, :] * _dot(y, wo_ref[...])

    ms = jnp.mean(x * x, axis=-1, keepdims=True)
    h = x * lax.rsqrt(ms + NORM_EPS) * g_ref[...]
    hb = (h * (1.0 + mod_ref[0, 4:5, :]) + mod_ref[0, 3:4, :]).astype(BF16)
    d_ff = w1_ref.shape[1]
    acc = jnp.zeros(x.shape, F32)
    for c in range(d_ff // ff_tile):
        cols = slice(c * ff_tile, (c + 1) * ff_tile)
        a = jnp.maximum(_dot(hb, w1_ref[:, cols]), 0.0)
        acc = acc + _dot((a * a).astype(BF16), w2_ref[cols, :])
    y = x + mod_ref[0, 5:6, :] * acc
    if final_norm:
        y = y * lax.rsqrt(jnp.mean(y * y, axis=-1, keepdims=True) + NORM_EPS) * fg_ref[...]
    o_ref[0] = y


def _post(x, mod, scans, gates, y_attn, w_out, norm_g, w1, w2, final_g, *, tm, final_norm):
    bsz, n, d = x.shape
    tm = min(tm, n)
    assert n % tm == 0 and w1.shape[1] % POST_FF_TILE == 0
    tok = lambda w: pl.BlockSpec((1, tm, w), lambda b, i: (b, i, 0))
    full = lambda a: pl.BlockSpec(a.shape, lambda b, i: (0,) * a.ndim, pipeline_mode=pl.Buffered(1))
    rf, rb, gf, gb = scans
    rg, gg = gates
    return pl.pallas_call(
        functools.partial(_post_kernel, ff_tile=POST_FF_TILE, final_norm=final_norm),
        grid=(bsz, n // tm),
        in_specs=[tok(d), pl.BlockSpec((1, 6, d), lambda b, i: (b, 0, 0)),
                  tok(256), tok(256), tok(256), tok(256), tok(256), tok(256), tok(512),
                  full(w_out), full(norm_g), full(w1), full(w2), full(final_g)],
        out_specs=tok(d),
        out_shape=jax.ShapeDtypeStruct((bsz, n, d), F32),
        compiler_params=pltpu.CompilerParams(dimension_semantics=("parallel", "parallel"),
                                             vmem_limit_bytes=V7X_VMEM_LIMIT_BYTES),
        name="post_final" if final_norm else "post",
    )(x, mod, rf, rb, rg, gf, gb, gg, y_attn, w_out, norm_g, w1, w2, final_g)


def _prep_w_in(w):
    d = w.shape[0]
    ret = w[:, 0:768]
    gla = w[:, 768:1536]
    low = w[:, 1536:1568]
    aq = w[:, 1568:2080]
    ak = w[:, 2080:2208]
    av = w[:, 2208:2336]
    w_nat = jnp.concatenate([ret, gla, low, jnp.zeros((d, 128 - 2 * GLA_RANK), w.dtype)], axis=1)
    w_tr = jnp.concatenate([aq, ak, av], axis=1).T
    return w_nat.astype(BF16), w_tr.astype(BF16)


def _rope_tables(n_lat):
    t = jnp.arange(n_lat)
    row = (t // GRID_W).astype(F32)
    col = (t % GRID_W).astype(F32)

    def head_tables(head_dim):
        nf = head_dim // 4
        inv_freq = ROPE_BASE ** (-jnp.arange(nf, dtype=F32) / nf)
        ang_r = row[:, None] * inv_freq
        ang_c = col[:, None] * inv_freq
        cos = jnp.concatenate([jnp.cos(ang_r)] * 2 + [jnp.cos(ang_c)] * 2, axis=1)
        sin = jnp.concatenate([-jnp.sin(ang_r), jnp.sin(ang_r), -jnp.sin(ang_c), jnp.sin(ang_c)], axis=1)
        return cos, sin

    cos_r, sin_r = head_tables(RET_DK)
    cos_r = jnp.tile(cos_r, (1, RET_HEADS))
    sin_r = jnp.tile(sin_r, (1, RET_HEADS))
    cos_a, sin_a = head_tables(GQA_HEAD_DIM)
    return cos_r, sin_r, cos_a.T, sin_a.T


def kernel(x, c, ctx, c_ctx, mod_w, mod_b, attn_norm_g, mlp_norm_g, w_in, w_out, ret_decay_logit,
           gla_gate_w, gla_gate_b, qk_norm_g, mlp_w1, mlp_w2, final_norm_g):
    bsz, n_lat, d = x.shape
    n_ctx = ctx.shape[1]
    depth = mod_w.shape[0]
    scan_tile = 2048
    assert n_lat % scan_tile == 0 and n_ctx % SCAN_CHUNK == 0 and bsz <= 7

    cvec = jnp.zeros((8, d), F32).at[:bsz].set(c).at[bsz].set(c_ctx)
    mods = _modulation(cvec, mod_w, mod_b)
    lat_tables = _rope_tables(n_lat)
    ctx_tables = tuple(jnp.zeros((n_ctx, 128), F32) for _ in range(2)) + \
        tuple(jnp.zeros((GQA_HEAD_DIM, n_ctx), F32) for _ in range(2))
    final_g = final_norm_g.reshape(1, d)

    for i in range(depth):
        need_ctx = i < depth - 1
        mod_lat = mods[i, :bsz].reshape(bsz, 6, d)
        mod_ctx = jnp.broadcast_to(mods[i, bsz].reshape(1, 6, d), (bsz, 6, d))
        w_nat, w_tr = _prep_w_in(w_in[i])
        gw = jnp.zeros((128, 256), F32)
        gw = gw.at[0:GLA_RANK, 0:128].set(gla_gate_w[i, 0]).at[GLA_RANK:2 * GLA_RANK, 128:256].set(gla_gate_w[i, 1])
        gw = gw.astype(BF16)
        gb = gla_gate_b[i].reshape(1, 256)
        q_gain = qk_norm_g[i, 0].reshape(GQA_HEAD_DIM, 1)
        k_gain = qk_norm_g[i, 1].reshape(GQA_HEAD_DIM, 1)
        attn_g = attn_norm_g[i].reshape(1, d)
        mlp_g = mlp_norm_g[i].reshape(1, d)
        decay = jnp.repeat(ret_decay_logit[i], RET_DK, axis=1)
        w_o = w_out[i].astype(BF16)
        w1 = mlp_w1[i].astype(BF16)
        w2 = mlp_w2[i].astype(BF16)

        pl_lat = _in_proj(x, mod_lat, attn_g, w_nat, w_tr, gw, gb, q_gain, k_gain, lat_tables, rope=True, tm=1024)
        pl_ctx = _in_proj(ctx, mod_ctx, attn_g, w_nat, w_tr, gw, gb, q_gain, k_gain, ctx_tables, rope=False,
                          tm=n_ctx)
        ret = _bidir_scan(pl_lat[0:3], pl_ctx[0:3], None, None, decay, tile=scan_tile)
        gla = _bidir_scan(pl_lat[4:7], pl_ctx[4:7], pl_lat[8:10], pl_ctx[8:10], None, tile=scan_tile)
        ak = jnp.concatenate([pl_lat[11], pl_ctx[11]], axis=1)
        av = jnp.concatenate([pl_lat[12], pl_ctx[12]], axis=3)
        y_attn = _attention(pl_lat[10], ak, av, tq=2048, tk=3328)

        x = _post(x, mod_lat, (ret[0], ret[1], gla[0], gla[1]), (pl_lat[3], pl_lat[7]), y_attn, w_o,
                  mlp_g, w1, w2, final_g, tm=512, final_norm=not need_ctx)
        if need_ctx:
            y_attn_c = _attention(pl_ctx[10], pl_ctx[11], pl_ctx[12], tq=256, tk=256)
            ctx = _post(ctx, mod_ctx, (ret[2], ret[3], gla[2], gla[3]), (pl_ctx[3], pl_ctx[7]), y_attn_c, w_o,
                        mlp_g, w1, w2, final_g, tm=n_ctx, final_norm=False)
    return x
```
